```python
import math
import jax, jax.numpy as jnp
from jax import lax
import numpy as np

D_MODEL = 4096
BATCH = 4
SEQ = 2048
DEPTH = 2
DEC_BATCH = 8
DEC_SEQ = 4
PAST_LEN = 16384
PAGE_SIZE = 128

F32 = jnp.float32
N_META = 16
NORM_EPS = 1e-6
N_EVEN = (DEPTH + 1) // 2
N_ODD = DEPTH // 2

GLA_WIDTH = D_MODEL // 2
GLA_HEADS = 4
GLA_DK = GLA_WIDTH // 2 // GLA_HEADS
GLA_DV = GLA_WIDTH // GLA_HEADS
GLA_GATE_RANK = 16
GLA_TAU = 16.0
HGRN_WIDTH = D_MODEL // 2
HGRN_EXPAND = 128
HGRN_HEADS = HGRN_WIDTH // HGRN_EXPAND
HGRN_DK = HGRN_EXPAND
HGRN_DV = HGRN_WIDTH // HGRN_HEADS
CHUNK = 16
IN_SIZES = (GLA_HEADS * GLA_DK, GLA_HEADS * GLA_DK, GLA_HEADS * GLA_DV, GLA_GATE_RANK, GLA_HEADS * GLA_DV,
            HGRN_HEADS * HGRN_DK, HGRN_HEADS * HGRN_DK, HGRN_HEADS * HGRN_DV, HGRN_HEADS * HGRN_DV)
IN_COLS = sum(IN_SIZES)
IN_SPLITS = np.cumsum(IN_SIZES)[:-1].tolist()
DIFF_HEADS = 16
DIFF_DH = D_MODEL // (2 * DIFF_HEADS)
ROPE_THETA = 10000.0
Q_BLOCK = 128
D_FF = 14336
N_EXPERTS = 8
TOP_K = 2
EXPERT_FF = 14336

kernel_name = 'hybrid_gla_hgrn2_diffattn_moe_step'


def rmsnorm(x, w):
    xf = x.astype(F32)
    y = xf * lax.rsqrt(jnp.mean(xf * xf, axis=-1, keepdims=True) + NORM_EPS) * w.astype(F32)
    return y.astype(x.dtype)


def swiglu(x, w_gate, w_up, w_down):
    return (jax.nn.silu(x @ w_gate) * (x @ w_up)) @ w_down


def rope(x, pos):
    dh = x.shape[-1]
    half = dh // 2
    inv = ROPE_THETA ** (-jnp.arange(half, dtype=F32) * 2.0 / dh)
    ang = pos.astype(F32)[:, None] * inv[None, :]
    cos = jnp.cos(ang)[None, :, None, :]
    sin = jnp.sin(ang)[None, :, None, :]
    xf = x.astype(F32)
    x1, x2 = xf[..., :half], xf[..., half:]
    return jnp.concatenate([x1 * cos - x2 * sin, x1 * sin + x2 * cos], axis=-1).astype(x.dtype)


def chunked_gla(q, k, v, log_f, s0):
    b, l, h, _ = q.shape
    pad = (-l) % CHUNK
    q, k, v, log_f = [t.astype(F32) for t in (q, k, v, log_f)]
    if pad:
        pw = ((0, 0), (0, pad), (0, 0), (0, 0))
        q, k, v, log_f = [jnp.pad(t, pw) for t in (q, k, v, log_f)]
    n = (l + pad) // CHUNK

    def to_chunks(t):
        return t.reshape(b, n, CHUNK, h, t.shape[-1]).transpose(1, 0, 3, 2, 4)

    causal = jnp.tril(jnp.ones((CHUNK, CHUNK), bool))[None, None, :, :, None]

    def step(s, inp):
        qi, ki, vi, gi = inp
        cum = jnp.cumsum(gi, axis=2)
        o_inter = jnp.einsum('bhcd,bhde->bhce', qi * jnp.exp(cum), s)
        rel = jnp.where(causal, cum[:, :, :, None, :] - cum[:, :, None, :, :], -jnp.inf)
        att = jnp.einsum('bhid,bhjd,bhijd->bhij', qi, ki, jnp.exp(rel))
        o = o_inter + jnp.einsum('bhij,bhje->bhie', att, vi)
        last = cum[:, :, -1:, :]
        s = jnp.exp(last[:, :, 0, :, None]) * s + jnp.einsum('bhcd,bhce->bhde', ki * jnp.exp(last - cum), vi)
        return s, o

    s_fin, o = lax.scan(step, s0.astype(F32), (to_chunks(q), to_chunks(k), to_chunks(v), to_chunks(log_f)))
    o = o.transpose(1, 0, 3, 2, 4).reshape(b, n * CHUNK, h, -1)[:, :l]
    return o, s_fin.astype(s0.dtype)


def gla_hgrn_mixer(xn, s_gla, s_hgrn, lb, w_in, gla_w_gate_up, gla_b_gate, gla_norm, hgrn_norm, w_out):
    b, l, _ = xn.shape
    gq, gk, gv, g_low, g_og, hq, hf, hi, h_og = jnp.split(xn @ w_in, IN_SPLITS, axis=-1)

    def heads(t, nh):
        return t.reshape(b, l, nh, -1)

    g_log = jax.nn.log_sigmoid((g_low @ gla_w_gate_up + gla_b_gate).astype(F32)) / GLA_TAU
    o_a, s_gla = chunked_gla(heads(gq, GLA_HEADS) * GLA_DK ** -0.5, heads(gk, GLA_HEADS),
                             heads(gv, GLA_HEADS), heads(g_log, GLA_HEADS), s_gla)
    o_a = rmsnorm(o_a.astype(xn.dtype), gla_norm) * jax.nn.silu(heads(g_og, GLA_HEADS))
    f = lb + (1.0 - lb) * jax.nn.sigmoid(hf.astype(F32))
    o_b, s_hgrn = chunked_gla(heads(jax.nn.silu(hq), HGRN_HEADS), heads(1.0 - f, HGRN_HEADS),
                              heads(hi, HGRN_HEADS), heads(jnp.log(f), HGRN_HEADS), s_hgrn)
    o_b = rmsnorm(o_b.astype(xn.dtype), hgrn_norm) * jax.nn.silu(heads(h_og, HGRN_HEADS))
    o = jnp.concatenate([o_a.reshape(b, l, -1), o_b.reshape(b, l, -1)], axis=-1)
    return o @ w_out, s_gla, s_hgrn


def diff_attend_prompt(q, k, v, lam):
    b, l, g, dh = q.shape
    h = g // 2
    nq = -(-l // Q_BLOCK)
    qp = jnp.pad(q, ((0, 0), (0, nq * Q_BLOCK - l), (0, 0), (0, 0)))
    kpos = jnp.arange(l)
    scale = dh ** -0.5

    def block(i):
        qb = lax.dynamic_slice_in_dim(qp, i * Q_BLOCK, Q_BLOCK, axis=1)
        qpos = i * Q_BLOCK + jnp.arange(Q_BLOCK)
        s = jnp.einsum('bqgd,bkgd->bgqk', qb, k, preferred_element_type=F32) * scale
        s = jnp.where(kpos[None, None, None, :] <= qpos[None, None, :, None], s, -jnp.inf)
        p = jax.nn.softmax(s, axis=-1).reshape(b, h, 2, Q_BLOCK, l)
        a = p[:, :, 0] - lam * p[:, :, 1]
        return jnp.einsum('bhqk,bkhe->bqhe', a.astype(v.dtype), v)

    o = lax.map(block, jnp.arange(nq))
    return o.transpose(1, 0, 2, 3, 4).reshape(b, nq * Q_BLOCK, h, -1)[:, :l]


def diff_attend_paged(q, k_new, v_new, cache_k, cache_v, page_table, layer, lam):
    b, t, g, dh = q.shape
    h = g // 2
    page = cache_k.shape[2]
    qf = q.astype(F32).reshape(b, t, h, 2, dh) * dh ** -0.5

    def merge(carry, s, vals):
        m, den, acc = carry
        m_new = jnp.maximum(m, s.max(axis=-1))
        corr = jnp.exp(m - m_new)
        p = jnp.exp(s - m_new[..., None])
        acc = acc * corr[..., None] + jnp.einsum('bhrts,bshe->bhrte', p, vals.astype(F32))
        return (m_new, den * corr + p.sum(axis=-1), acc)

    def page_step(carry, phys):
        kp = cache_k[layer, phys].astype(F32).reshape(b, page, h, 2, dh)
        s = jnp.einsum('bthrd,bshrd->bhrts', qf, kp)
        return merge(carry, s, cache_v[layer, phys]), None

    init = (jnp.full((b, h, 2, t), -jnp.inf, F32), jnp.zeros((b, h, 2, t), F32),
            jnp.zeros((b, h, 2, t, 2 * dh), F32))
    carry, _ = lax.scan(page_step, init, page_table.T)
    s_self = jnp.einsum('bthrd,bshrd->bhrts', qf, k_new.astype(F32).reshape(b, t, h, 2, dh))
    s_self = jnp.where(jnp.tril(jnp.ones((t, t), bool)), s_self, -jnp.inf)
    _, den, acc = merge(carry, s_self, v_new)
    o = acc / den[..., None]
    o = o[:, :, 0] - lam * o[:, :, 1]
    return o.transpose(0, 2, 1, 3).astype(v_new.dtype)


def diff_attention(xn, pos, paged, layer, lam_init, w_q, w_k, w_v, lq1, lk1, lq2, lk2, head_norm, w_o):
    b, l, _ = xn.shape
    q = rope((xn @ w_q).reshape(b, l, 2 * DIFF_HEADS, DIFF_DH), pos)
    k = rope((xn @ w_k).reshape(b, l, 2 * DIFF_HEADS, DIFF_DH), pos)
    v = (xn @ w_v).reshape(b, l, DIFF_HEADS, 2 * DIFF_DH)
    lam = (jnp.exp(jnp.sum(lq1.astype(F32) * lk1.astype(F32)))
           - jnp.exp(jnp.sum(lq2.astype(F32) * lk2.astype(F32))) + lam_init)
    if paged is None:
        o = diff_attend_prompt(q, k, v, lam)
    else:
        cache_k, cache_v, page_table = paged
        o = diff_attend_paged(q, k, v, cache_k, cache_v, page_table, layer, lam)
    o = rmsnorm(o, head_norm) * (1.0 - lam_init)
    return o.reshape(b, l, -1) @ w_o, k, v


def moe_swiglu(xn, w_router, w_gate, w_up, w_down, layer):
    logits = jnp.einsum('bld,de->ble', xn, w_router, preferred_element_type=F32)
    top_val, top_idx = lax.top_k(logits, TOP_K)
    probs = jax.nn.softmax(top_val, axis=-1)
    gate = jnp.sum(jax.nn.one_hot(top_idx, N_EXPERTS, dtype=F32) * probs[..., None], axis=-2).astype(xn.dtype)
    out = jnp.zeros_like(xn)
    for e in range(N_EXPERTS):
        out = out + gate[..., e:e + 1] * swiglu(xn, w_gate[layer, e], w_up[layer, e], w_down[layer, e])
    return out


def trunk(x, pos, s_gla0, s_hgrn0, paged, norm_mix, norm_ffn, final_norm, a_w_in, a_gla_w_gate_up,
          a_gla_b_gate, a_gla_norm, a_hgrn_lb_logits, a_hgrn_norm, a_w_out, c_w_q, c_w_k, c_w_v,
          c_lambda_q1, c_lambda_k1, c_lambda_q2, c_lambda_k2, c_head_norm, c_w_o, ffn_w_gate, ffn_w_up,
          ffn_w_down, moe_w_router, moe_w_gate, moe_w_up, moe_w_down):
    lbs = jnp.cumsum(jax.nn.softmax(a_hgrn_lb_logits.astype(F32), axis=0), axis=0)
    gla_states, hgrn_states, k_rows, v_rows = [], [], [], []
    for i in range(DEPTH):
        j = i // 2
        xn = rmsnorm(x, norm_mix[i])
        if i % 2 == 0:
            mix, sg, sh = gla_hgrn_mixer(xn, s_gla0[j], s_hgrn0[j], lbs[j], a_w_in[j], a_gla_w_gate_up[j],
                                         a_gla_b_gate[j], a_gla_norm[j], a_hgrn_norm[j], a_w_out[j])
            gla_states.append(sg)
            hgrn_states.append(sh)
            x = x + mix
            x = x + swiglu(rmsnorm(x, norm_ffn[i]), ffn_w_gate[j], ffn_w_up[j], ffn_w_down[j])
        else:
            lam_init = 0.8 - 0.6 * math.exp(-0.3 * i)
            mix, kr, vr = diff_attention(xn, pos, paged, j, lam_init, c_w_q[j], c_w_k[j], c_w_v[j],
                                         c_lambda_q1[j], c_lambda_k1[j], c_lambda_q2[j], c_lambda_k2[j],
                                         c_head_norm[j], c_w_o[j])
            k_rows.append(kr)
            v_rows.append(vr)
            x = x + mix
            x = x + moe_swiglu(rmsnorm(x, norm_ffn[i]), moe_w_router[j], moe_w_gate, moe_w_up, moe_w_down, j)
    return (rmsnorm(x, final_norm), jnp.stack(gla_states), jnp.stack(hgrn_states),
            jnp.stack(k_rows), jnp.stack(v_rows))


def setup_inputs(seed: int = 0) -> dict:
    key = jax.random.key(seed)
    ks = iter(jax.random.split(key, 48))

    def nrm(shape, scale):
        return scale * jax.random.normal(next(ks), shape, F32)

    def gain(shape):
        return 1.0 + nrm(shape, 0.01)

    n_pages = PAST_LEN // PAGE_SIZE
    n_used = DEC_BATCH * n_pages
    n_pool = n_used + max(1, n_used // 4)
    page_table = jax.random.permutation(next(ks), n_pool)[:n_used].reshape(DEC_BATCH, n_pages).astype(jnp.int32)
    ds = D_MODEL ** -0.5
    return {
        'x_prompt': nrm((BATCH, SEQ, D_MODEL), 1.0),
        'x_sample': nrm((DEC_BATCH, DEC_SEQ, D_MODEL), 1.0),
        'state_gla': nrm((N_EVEN, DEC_BATCH, GLA_HEADS, GLA_DK, GLA_DV), 1.0),
        'state_hgrn': nrm((N_EVEN, DEC_BATCH, HGRN_HEADS, HGRN_DK, HGRN_DV), 0.5),
        'cache_k': nrm((N_ODD, n_pool, PAGE_SIZE, 2 * DIFF_HEADS, DIFF_DH), 1.0),
        'cache_v': nrm((N_ODD, n_pool, PAGE_SIZE, DIFF_HEADS, 2 * DIFF_DH), 1.0),
        'page_table': page_table,
        'meta_tokens': nrm((N_META, D_MODEL), 1.0),
        'norm_mix': gain((DEPTH, D_MODEL)),
        'norm_ffn': gain((DEPTH, D_MODEL)),
        'final_norm': gain((D_MODEL,)),
        'a_w_in': nrm((N_EVEN, D_MODEL, IN_COLS), ds),
        'a_gla_w_gate_up': nrm((N_EVEN, GLA_GATE_RANK, GLA_HEADS * GLA_DK), GLA_GATE_RANK ** -0.5),
        'a_gla_b_gate': nrm((N_EVEN, GLA_HEADS * GLA_DK), 0.1),
        'a_gla_norm': gain((N_EVEN, GLA_DV)),
        'a_hgrn_lb_logits': nrm((N_EVEN + 1, HGRN_HEADS * HGRN_DK), 0.1),
        'a_hgrn_norm': gain((N_EVEN, HGRN_DV)),
        'a_w_out': nrm((N_EVEN, D_MODEL, D_MODEL), ds),
        'c_w_q': nrm((N_ODD, D_MODEL, 2 * DIFF_HEADS * DIFF_DH), ds),
        'c_w_k': nrm((N_ODD, D_MODEL, 2 * DIFF_HEADS * DIFF_DH), ds),
        'c_w_v': nrm((N_ODD, D_MODEL, 2 * DIFF_HEADS * DIFF_DH), ds),
        'c_lambda_q1': nrm((N_ODD, DIFF_DH), 0.1),
        'c_lambda_k1': nrm((N_ODD, DIFF_DH), 0.1),
        'c_lambda_q2': nrm((N_ODD, DIFF_DH), 0.1),
        'c_lambda_k2': nrm((N_ODD, DIFF_DH), 0.1),
        'c_head_norm': gain((N_ODD, 2 * DIFF_DH)),
        'c_w_o': nrm((N_ODD, 2 * DIFF_HEADS * DIFF_DH, D_MODEL), ds),
        'ffn_w_gate': nrm((N_EVEN, D_MODEL, D_FF), ds),
        'ffn_w_up': nrm((N_EVEN, D_MODEL, D_FF), ds),
        'ffn_w_down': nrm((N_EVEN, D_FF, D_MODEL), D_FF ** -0.5),
        'moe_w_router': nrm((N_ODD, D_MODEL, N_EXPERTS), ds),
        'moe_w_gate': nrm((N_ODD, N_EXPERTS, D_MODEL, EXPERT_FF), ds),
        'moe_w_up': nrm((N_ODD, N_EXPERTS, D_MODEL, EXPERT_FF), ds),
        'moe_w_down': nrm((N_ODD, N_EXPERTS, EXPERT_FF, D_MODEL), EXPERT_FF ** -0.5),
    }


def reference(x_prompt, x_sample, state_gla, state_hgrn, cache_k, cache_v, page_table, meta_tokens,
              norm_mix, norm_ffn, final_norm, a_w_in, a_gla_w_gate_up, a_gla_b_gate, a_gla_norm,
              a_hgrn_lb_logits, a_hgrn_norm, a_w_out, c_w_q, c_w_k, c_w_v, c_lambda_q1, c_lambda_k1,
              c_lambda_q2, c_lambda_k2, c_head_norm, c_w_o, ffn_w_gate, ffn_w_up, ffn_w_down,
              moe_w_router, moe_w_gate, moe_w_up, moe_w_down):
    weights = (norm_mix, norm_ffn, final_norm, a_w_in, a_gla_w_gate_up, a_gla_b_gate, a_gla_norm,
               a_hgrn_lb_logits, a_hgrn_norm, a_w_out, c_w_q, c_w_k, c_w_v, c_lambda_q1, c_lambda_k1,
               c_lambda_q2, c_lambda_k2, c_head_norm, c_w_o, ffn_w_gate, ffn_w_up, ffn_w_down,
               moe_w_router, moe_w_gate, moe_w_up, moe_w_down)
    b = x_prompt.shape[0]
    meta = jnp.broadcast_to(meta_tokens[None].astype(x_prompt.dtype), (b, N_META, x_prompt.shape[-1]))
    xp = jnp.concatenate([meta, x_prompt], axis=1)
    pos_p = jnp.arange(xp.shape[1], dtype=jnp.int32)
    zg = jnp.zeros((N_EVEN, b) + state_gla.shape[2:], state_gla.dtype)
    zh = jnp.zeros((N_EVEN, b) + state_hgrn.shape[2:], state_hgrn.dtype)
    yp, sg_p, sh_p, k_p, v_p = trunk(xp, pos_p, zg, zh, None, *weights)
    past_len = page_table.shape[1] * cache_k.shape[2]
    pos_s = past_len + jnp.arange(x_sample.shape[1], dtype=jnp.int32)
    ys, sg_s, sh_s, k_s, v_s = trunk(x_sample, pos_s, state_gla, state_hgrn,
                                     (cache_k, cache_v, page_table), *weights)
    return (yp[:, N_META:], ys, sg_p, sh_p, k_p, v_p, sg_s, sh_s, k_s, v_s)
```

```python
import functools
import math

import jax
import jax.numpy as jnp
from jax import lax
from jax.experimental import pallas as pl
from jax.experimental.pallas import tpu as pltpu

F32 = jnp.float32
BF16 = jnp.bfloat16
NORM_EPS = 1e-6
ROPE_THETA = 10000.0
GLA_TAU = 16.0
LANES = 128
SAMPLE_ROWS = 8
SUB = 16
MOE_ROW_TILE = 256
VMEM_CAP = 60 * 2 ** 20
VMEM_MIN = 32 * 2 ** 20
NT = (((1,), (1,)), ((), ()))
TN = (((0,), (0,)), ((), ()))


def _cparams(semantics, est_bytes):
    limit = int(min(max(est_bytes * 5 // 4 + (4 << 20), VMEM_MIN), VMEM_CAP))
    return pltpu.CompilerParams(dimension_semantics=semantics, vmem_limit_bytes=limit)


def _pick(n, cands):
    for c in cands:
        if n % c == 0:
            return c
    raise ValueError(f"no tile for {n} in {cands}")


def _split_bf16(x, parts):
    out = []
    for _ in range(parts):
        h = x.astype(BF16)
        out.append(h)
        x = x - h.astype(F32)
    return out


def _dot_f32(a, b):
    a1, a2 = _split_bf16(a, 2)
    b1, b2 = _split_bf16(b, 2)
    d = functools.partial(jnp.dot, preferred_element_type=F32)
    return d(a1, b1) + (d(a1, b2) + d(a2, b1))


def _rms(x, w):
    return x * lax.rsqrt(jnp.mean(x * x, axis=-1, keepdims=True) + NORM_EPS) * w


def _silu(x):
    return x * jax.nn.sigmoid(x)


def _norm_kernel(*refs, n_delta, want_x, n_exp):
    it = iter(refs)
    x_ref = next(it)
    d_refs = [next(it) for _ in range(n_delta)]
    w_ref = next(it)
    wr_ref = next(it) if n_exp else None
    xo_ref = next(it) if want_x else None
    no_ref = next(it)
    x = x_ref[...]
    if n_delta == 1:
        x = x + d_refs[0][...]
    elif n_delta == 2:
        x = x + (d_refs[0][...] + d_refs[1][...])
    if want_x:
        xo_ref[...] = x
    y = _rms(x, w_ref[...])
    no_ref[...] = y.astype(no_ref.dtype)
    if n_exp:
        ti_ref, tp_ref = next(it), next(it)
        logits = _dot_f32(y, wr_ref[...])
        col = lax.broadcasted_iota(jnp.int32, logits.shape, 1)
        colf = col.astype(F32)
        logits = jnp.where(col < n_exp, logits, -jnp.inf)
        m1 = jnp.max(logits, axis=1, keepdims=True)
        i1 = jnp.min(jnp.where(logits == m1, colf, float(LANES)), axis=1, keepdims=True)
        l2 = jnp.where(colf == i1, -jnp.inf, logits)
        m2 = jnp.max(l2, axis=1, keepdims=True)
        i2 = jnp.min(jnp.where(l2 == m2, colf, float(LANES)), axis=1, keepdims=True)
        e = jnp.exp(m2 - m1)
        den = 1.0 + e
        ti_ref[...] = jnp.where(col == 0, i1, jnp.where(col == 1, i2, 0.0))
        tp_ref[...] = jnp.where(col == 0, 1.0 / den, jnp.where(col == 1, e / den, 0.0))


def _norm(x, deltas, w, *, want_x, out_dtype, w_router=None):
    R, D = x.shape
    tr = _pick(R, (208, 160, 144, 136, 128, 64, 48, 32, 16))
    n_exp = 0 if w_router is None else w_router.shape[1]
    row = pl.BlockSpec((tr, D), lambda i: (i, 0))
    args = [x, *deltas, w.reshape(1, D).astype(F32)]
    in_specs = [row] * (1 + len(deltas)) + [pl.BlockSpec((1, D), lambda i: (0, 0))]
    if n_exp:
        args.append(jnp.pad(w_router.astype(F32), ((0, 0), (0, LANES - n_exp))))
        in_specs.append(pl.BlockSpec((D, LANES), lambda i: (0, 0)))
    out_shape, out_specs = [], []
    if want_x:
        out_shape.append(jax.ShapeDtypeStruct((R, D), F32))
        out_specs.append(row)
    out_shape.append(jax.ShapeDtypeStruct((R, D), out_dtype))
    out_specs.append(row)
    if n_exp:
        lane = pl.BlockSpec((tr, LANES), lambda i: (i, 0))
        out_shape += [jax.ShapeDtypeStruct((R, LANES), F32)] * 2
        out_specs += [lane, lane]
    est = 2 * tr * D * 4 * (3 + len(deltas)) + (D * LANES * 8 if n_exp else 0) + 4 * tr * D * 4
    return pl.pallas_call(
        functools.partial(_norm_kernel, n_delta=len(deltas), want_x=want_x, n_exp=n_exp),
        grid=(R // tr,), in_specs=in_specs, out_specs=out_specs, out_shape=out_shape,
        compiler_params=_cparams(("arbitrary",), est), name="norm")(*args)


def _gmm_kernel(te_ref, nv_ref, *refs, n_w, epilogue, has_partial, has_scale):
    it = iter(refs)
    a_ref = next(it)
    w_refs = [next(it) for _ in range(n_w)]
    p_ref = next(it) if has_partial else None
    s_ref = next(it) if has_scale else None
    cos_ref, sin_ref = (next(it), next(it)) if epilogue == "rope" else (None, None)
    o_ref = next(it)
    wbf = [next(it) for _ in range(n_w)]
    i = pl.program_id(1)
    is_new = jnp.logical_or(i == 0, te_ref[i] != te_ref[jnp.maximum(i - 1, 0)])

    @pl.when(is_new)
    def _():
        for w_ref, s in zip(w_refs, wbf):
            s[...] = w_ref[0].astype(BF16)

    @pl.when(i < nv_ref[0])
    def _():
        a = a_ref[...]
        acc = jnp.dot(a, wbf[0][...], preferred_element_type=F32)
        if epilogue == "swiglu":
            acc = _silu(acc) * jnp.dot(a, wbf[1][...], preferred_element_type=F32)
        if has_partial:
            acc = p_ref[...] + acc
        if has_scale:
            acc = acc * s_ref[...]
        if epilogue == "rope":
            cos, sin = cos_ref[...], sin_ref[...]
            for g in range(acc.shape[1] // LANES):
                t = acc[:, g * LANES:(g + 1) * LANES]
                o_ref[:, g * LANES:(g + 1) * LANES] = (
                    t * cos + pltpu.roll(t, LANES // 2, 1) * sin).astype(o_ref.dtype)
        else:
            o_ref[...] = acc.astype(o_ref.dtype)

    @pl.when(i >= nv_ref[0])
    def _():
        o_ref[...] = jnp.zeros_like(o_ref)


def _gmm(a, ws, *, tm, tn, n_out=None, w_col0=0, kb=0, k_blk=None, te=None, nv=None,
         epilogue="plain", out_dtype=F32, partial=None, scale=None, rope=None, name="gmm"):
    R = a.shape[0]
    ws = [w if w.ndim == 3 else w[None] for w in ws]
    k_blk = a.shape[1] if k_blk is None else k_blk
    n_out = ws[0].shape[2] if n_out is None else n_out
    assert R % tm == 0 and n_out % tn == 0 and w_col0 % tn == 0
    T = R // tm
    if te is None:
        te = jnp.zeros((T,), jnp.int32)
        nv = jnp.full((1,), T, jnp.int32)
    c0 = w_col0 // tn

    def a_map(j, i, te_r, nv_r):
        return (jnp.minimum(i, nv_r[0] - 1), kb)

    def w_map(j, i, te_r, nv_r):
        return (te_r[i], kb, j + c0)

    def o_map(j, i, te_r, nv_r):
        return (i, j)

    def r_map(j, i, te_r, nv_r):
        return (i, 0)

    args = [a, *ws]
    in_specs = [pl.BlockSpec((tm, k_blk), a_map)] + [pl.BlockSpec((1, k_blk, tn), w_map)] * len(ws)
    est = 2 * tm * k_blk * a.dtype.itemsize + len(ws) * k_blk * max(tn, LANES) * (2 * 4 + 2)
    est += tm * tn * (2 * jnp.dtype(out_dtype).itemsize + 4 * (2 + len(ws)))
    if partial is not None:
        args.append(partial)
        in_specs.append(pl.BlockSpec((tm, tn), o_map))
        est += 2 * tm * tn * 4
    if scale is not None:
        args.append(scale.reshape(R, 1))
        in_specs.append(pl.BlockSpec((tm, 1), r_map))
        est += 2 * tm * LANES * 4
    if rope is not None:
        args += [rope[0], rope[1]]
        in_specs += [pl.BlockSpec((tm, LANES), r_map)] * 2
        est += 4 * tm * LANES * 4
    grid_spec = pltpu.PrefetchScalarGridSpec(
        num_scalar_prefetch=2, grid=(n_out // tn, T), in_specs=in_specs,
        out_specs=pl.BlockSpec((tm, tn), o_map),
        scratch_shapes=[pltpu.VMEM((k_blk, tn), BF16) for _ in ws])
    return pl.pallas_call(
        functools.partial(_gmm_kernel, n_w=len(ws), epilogue=epilogue,
                          has_partial=partial is not None, has_scale=scale is not None),
        grid_spec=grid_spec, out_shape=jax.ShapeDtypeStruct((R, n_out), out_dtype),
        compiler_params=_cparams(("arbitrary", "arbitrary"), est), name=name)(te, nv, *args)


def _rec_block(q, k, v, g, st_ref, sub):
    T, dk = q.shape
    rr = lax.broadcasted_iota(jnp.int32, (T, T), 0)
    cc = lax.broadcasted_iota(jnp.int32, (T, T), 1)
    tril = jnp.where(cc <= rr, 1.0, 0.0).astype(BF16)
    cum = sum(jnp.dot(tril, part, preferred_element_type=F32) for part in _split_bf16(g, 3))
    last = cum[T - 1:T, :]
    st = st_ref[0, 0]
    o_inter = lax.dot_general((q * jnp.exp(cum)).astype(BF16), st.astype(BF16), NT,
                              preferred_element_type=F32)
    k_end = (k * jnp.exp(last - cum)).astype(BF16)
    vb = v.astype(BF16)
    st_ref[0, 0] = st * jnp.exp(last) + lax.dot_general(vb, k_end, TN, preferred_element_type=F32)
    row_in_sub = lax.broadcasted_iota(jnp.int32, (sub, dk), 0)
    lane_in_sub = lax.broadcasted_iota(jnp.int32, (sub, sub), 1)
    outs = []
    for a in range(T // sub):
        lo, hi = a * sub, (a + 1) * sub
        qa, ka, ca = q[lo:hi], k[lo:hi], cum[lo:hi]
        att = jnp.zeros((sub, sub), F32)
        for j in range(sub):
            decay = jnp.where(row_in_sub >= j, jnp.exp(ca - ca[j:j + 1, :]), 0.0)
            col = jnp.sum(qa * ka[j:j + 1, :] * decay, axis=1, keepdims=True)
            att = jnp.where(lane_in_sub == j, col, att)
        oa = o_inter[lo:hi] + jnp.dot(att.astype(BF16), vb[lo:hi], preferred_element_type=F32)
        if a > 0:
            cb = cum[lo - 1:lo, :]
            qt = (qa * jnp.exp(ca - cb)).astype(BF16)
            kt = (k[:lo] * jnp.exp(cb - cum[:lo])).astype(BF16)
            att_prev = lax.dot_general(qt, kt, NT, preferred_element_type=F32)
            oa = oa + jnp.dot(att_prev.astype(BF16), vb[:lo], preferred_element_type=F32)
        outs.append(oa)
    return outs[0] if len(outs) == 1 else jnp.concatenate(outs, axis=0)


def _rec_kernel(*refs, mode, n_rows, blk, n_valid, has_init, slot):
    it = iter(refs)
    if mode == "gla":
        q_ref, k_ref, v_ref, gl_ref, wg_ref, bg_ref = (next(it) for _ in range(6))
    else:
        q_ref, f_ref, v_ref, lb_ref = (next(it) for _ in range(4))
    og_ref, nw_ref = next(it), next(it)
    s0_ref = next(it) if has_init else None
    o_ref, st_ref = next(it), next(it)
    dk = q_ref.shape[1]
    if has_init:
        st_ref[...] = s0_ref[...]
    else:
        st_ref[...] = jnp.zeros_like(st_ref)
    pad = blk - n_rows if n_rows < blk else 0

    def load(ref, r0):
        if pad:
            x = ref[...]
            return jnp.concatenate([x, jnp.zeros((pad, x.shape[1]), x.dtype)], axis=0)
        return ref[pl.ds(r0, blk), :]

    def body(ib, carry):
        r0 = pl.multiple_of(ib * blk, SUB)
        v = load(v_ref, r0)
        if mode == "gla":
            q = load(q_ref, r0) * dk ** -0.5
            k = load(k_ref, r0)
            z = _dot_f32(load(gl_ref, r0), wg_ref[...]) + bg_ref[...]
            g = (jnp.minimum(z, 0.0) - jnp.log1p(jnp.exp(-jnp.abs(z)))) / GLA_TAU
        else:
            lg = lb_ref[...]
            ex = jnp.exp(lg - jnp.max(lg, axis=0, keepdims=True))
            sm = ex / jnp.sum(ex, axis=0, keepdims=True)
            lb = jnp.sum(sm[:slot + 1], axis=0, keepdims=True)
            f = lb + (1.0 - lb) * jax.nn.sigmoid(load(f_ref, r0))
            q = _silu(load(q_ref, r0))
            k = 1.0 - f
            g = jnp.log(f)
        if n_valid < blk:
            live = lax.broadcasted_iota(jnp.int32, (blk, dk), 0) < n_valid
            k = jnp.where(live, k, 0.0)
            g = jnp.where(live, g, 0.0)
        o = _rec_block(q, k, v, g, st_ref, SUB)
        y = _rms(o, nw_ref[...]) * _silu(load(og_ref, r0))
        if pad:
            o_ref[...] = y[:n_rows].astype(o_ref.dtype)
        else:
            o_ref[pl.ds(r0, blk), :] = y.astype(o_ref.dtype)
        return carry

    n_blk = 1 if pad else n_rows // blk
    if n_blk == 1:
        body(0, 0)
    else:
        lax.fori_loop(0, n_blk, body, 0)


def _recurrence(mode, srcs, og_src, norm_w, s0t, *, n_batch, n_heads, n_rows, row0, n_valid, dk, dv,
                out_dtype, slot=0):
    rb0 = row0 // n_rows
    assert row0 % n_rows == 0
    blk = SUB if n_rows < SUB else _pick(n_rows, (64, 48, 32, 16))
    args, in_specs = [], []
    est = 0
    for src in list(srcs) + [og_src]:
        if len(src) == 3 and isinstance(src[1], int):
            arr, width, cb0 = src
            args.append(arr)
            in_specs.append(pl.BlockSpec((n_rows, width), lambda b, h, cb0=cb0: (rb0 + b, cb0 + h)))
            est += 2 * n_rows * max(width, LANES) * 4
        else:
            arr, shape, imap = src
            args.append(arr)
            in_specs.append(pl.BlockSpec(shape, imap))
    args.append(norm_w.reshape(1, dv).astype(F32))
    in_specs.append(pl.BlockSpec((1, dv), lambda b, h: (0, 0)))
    st_spec = pl.BlockSpec((1, 1, dv, dk), lambda b, h: (b, h, 0, 0))
    if s0t is not None:
        args.append(s0t)
        in_specs.append(st_spec)
    est += 4 * dv * dk * 4 + 2 * n_rows * dv * 4 + 64 * blk * max(dk, dv) * 4
    return pl.pallas_call(
        functools.partial(_rec_kernel, mode=mode, n_rows=n_rows, blk=blk, n_valid=n_valid,
                          has_init=s0t is not None, slot=slot),
        grid=(n_batch, n_heads), in_specs=in_specs,
        out_specs=[pl.BlockSpec((n_rows, dv), lambda b, h: (b, h)), st_spec],
        out_shape=[jax.ShapeDtypeStruct((n_batch * n_rows, n_heads * dv), out_dtype),
                   jax.ShapeDtypeStruct((n_batch, n_heads, dv, dk), F32)],
        compiler_params=_cparams(("arbitrary", "arbitrary"), est), name="rec_" + mode)(*args)


def _lambda(lam_ref, lam_init):
    lv = lam_ref[...]
    return (jnp.exp(jnp.sum(lv[0:1] * lv[1:2], axis=1, keepdims=True))
            - jnp.exp(jnp.sum(lv[2:3] * lv[3:4], axis=1, keepdims=True)) + lam_init)


def _attn_prompt_kernel(q_ref, k_ref, v_ref, lam_ref, hn_ref, o_ref, *, lam_init):
    tq, dh2 = q_ref.shape
    dh = dh2 // 2
    lp = k_ref.shape[1]
    lam = _lambda(lam_ref, lam_init)
    q = q_ref[...]
    k = k_ref[0]
    rows = pl.program_id(2) * tq + lax.broadcasted_iota(jnp.int32, (tq, lp), 0)
    causal = lax.broadcasted_iota(jnp.int32, (tq, lp), 1) <= rows
    probs = []
    for r in range(2):
        s = lax.dot_general(q[:, r * dh:(r + 1) * dh].astype(BF16), k[:, r * dh:(r + 1) * dh].astype(BF16),
                            NT, preferred_element_type=F32) * dh ** -0.5
        s = jnp.where(causal, s, -jnp.inf)
        e = jnp.exp(s - jnp.max(s, axis=1, keepdims=True))
        probs.append(e * (1.0 / jnp.sum(e, axis=1, keepdims=True)))
    a = probs[0] - lam * probs[1]
    o = jnp.dot(a.astype(BF16), v_ref[0].astype(BF16), preferred_element_type=F32)
    o_ref[...] = (_rms(o, hn_ref[...]) * (1.0 - lam_init)).astype(o_ref.dtype)


def _attn_prompt(q, kpad, vpad, lam_vec, head_norm, *, n_batch, seq, n_heads, dh, lam_init):
    lp = kpad.shape[1]
    tq = _pick(seq, (344, 256, 128, 64, 16))
    nq = seq // tq
    est = 4 * lp * 2 * dh * 4 + 4 * tq * 2 * dh * 4 + 8 * tq * lp * 4
    return pl.pallas_call(
        functools.partial(_attn_prompt_kernel, lam_init=lam_init),
        grid=(n_batch, n_heads, nq),
        in_specs=[pl.BlockSpec((tq, 2 * dh), lambda b, h, i: (b * nq + i, h)),
                  pl.BlockSpec((1, lp, 2 * dh), lambda b, h, i: (b, 0, h)),
                  pl.BlockSpec((1, lp, 2 * dh), lambda b, h, i: (b, 0, h)),
                  pl.BlockSpec((4, dh), lambda b, h, i: (0, 0)),
                  pl.BlockSpec((1, 2 * dh), lambda b, h, i: (0, 0))],
        out_specs=pl.BlockSpec((tq, 2 * dh), lambda b, h, i: (b * nq + i, h)),
        out_shape=jax.ShapeDtypeStruct((n_batch * seq, n_heads * 2 * dh), BF16),
        compiler_params=_cparams(("arbitrary",) * 3, est), name="attn_prompt")(
            q, kpad, vpad, lam_vec, head_norm.reshape(1, 2 * dh).astype(F32))


def _attn_paged_kernel(pt_ref, q_ref, kp_ref, vp_ref, kn_ref, vn_ref, lam_ref, hn_ref, o_ref,
                       qbd, m_s, d_s, acc, *, n_heads, dh, dec_seq, lam_init):
    p = pl.program_id(1)
    n_rows = qbd.shape[0]
    grp = 2 * SAMPLE_ROWS

    @pl.when(p == 0)
    def _():
        q8 = q_ref[...] * dh ** -0.5
        qt = jnp.concatenate([q8] * (2 * n_heads), axis=0)
        rg = lax.broadcasted_iota(jnp.int32, qt.shape, 0) >> (SAMPLE_ROWS.bit_length() - 1)
        cg = lax.broadcasted_iota(jnp.int32, qt.shape, 1) >> (dh.bit_length() - 1)
        qbd[...] = jnp.where(rg == cg, qt, 0.0).astype(BF16)
        m_s[...] = jnp.full_like(m_s, -jnp.inf)
        d_s[...] = jnp.zeros_like(d_s)
        acc[...] = jnp.zeros_like(acc)

    def merge(kt, vt, valid):
        s = lax.dot_general(qbd[...], kt.astype(BF16), NT, preferred_element_type=F32)
        if valid is not None:
            s = jnp.where(valid, s, -jnp.inf)
        m_old = m_s[...]
        m_new = jnp.maximum(m_old, jnp.max(s, axis=1, keepdims=True))
        corr = jnp.exp(m_old - m_new)
        pe = jnp.exp(s - m_new)
        d_s[...] = d_s[...] * corr + jnp.sum(pe, axis=1, keepdims=True)
        m_s[...] = m_new
        res = jnp.dot(pe.astype(BF16), vt.astype(BF16), preferred_element_type=F32)
        for h in range(n_heads):
            r = slice(h * grp, (h + 1) * grp)
            acc[r, :] = acc[r, :] * corr[r] + res[r, h * 2 * dh:(h + 1) * 2 * dh]

    merge(kp_ref[0], vp_ref[0], None)

    @pl.when(p == pl.num_programs(1) - 1)
    def _():
        zeros = jnp.zeros_like(kn_ref[...])
        kn = jnp.concatenate([kn_ref[...], zeros], axis=0)
        vn = jnp.concatenate([vn_ref[...], zeros], axis=0)
        tok = lax.broadcasted_iota(jnp.int32, (n_rows, 2 * SAMPLE_ROWS), 0) & (SAMPLE_ROWS - 1)
        key = lax.broadcasted_iota(jnp.int32, (n_rows, 2 * SAMPLE_ROWS), 1)
        merge(kn, vn, jnp.logical_and(key <= tok, key < dec_seq))
        lam = _lambda(lam_ref, lam_init)
        o = acc[...] / d_s[...]
        for h in range(n_heads):
            oh = o[h * grp:h * grp + SAMPLE_ROWS] - lam * o[h * grp + SAMPLE_ROWS:(h + 1) * grp]
            o_ref[:, h * 2 * dh:(h + 1) * 2 * dh] = _rms(oh, hn_ref[...]) * (1.0 - lam_init)


def _attn_paged(q, k, v, cache_k, cache_v, page_table, layer, lam_vec, head_norm, *, row0, n_heads, dh,
                dec_seq, lam_init):
    n_dec, n_pages = page_table.shape
    n_pool, page = cache_k.shape[1], cache_k.shape[2]
    dm = n_heads * 2 * dh
    ck = cache_k.reshape(-1, page, dm)
    cv = cache_v.reshape(-1, page, dm)
    rb0 = row0 // SAMPLE_ROWS
    n_rows = 2 * n_heads * SAMPLE_ROWS
    base = layer * n_pool

    def page_map(b, p, pt):
        return (base + pt[b * n_pages + p], 0, 0)

    def new_map(b, p, pt):
        return (rb0 + b, 0)

    grid_spec = pltpu.PrefetchScalarGridSpec(
        num_scalar_prefetch=1, grid=(n_dec, n_pages),
        in_specs=[pl.BlockSpec((SAMPLE_ROWS, dm), new_map),
                  pl.BlockSpec((1, page, dm), page_map), pl.BlockSpec((1, page, dm), page_map),
                  pl.BlockSpec((SAMPLE_ROWS, dm), new_map), pl.BlockSpec((SAMPLE_ROWS, dm), new_map),
                  pl.BlockSpec((4, dh), lambda b, p, pt: (0, 0)),
                  pl.BlockSpec((1, 2 * dh), lambda b, p, pt: (0, 0))],
        out_specs=pl.BlockSpec((SAMPLE_ROWS, dm), lambda b, p, pt: (b, 0)),
        scratch_shapes=[pltpu.VMEM((n_rows, dm), BF16), pltpu.VMEM((n_rows, 1), F32),
                        pltpu.VMEM((n_rows, 1), F32), pltpu.VMEM((n_rows, 2 * dh), F32)])
    est = 4 * page * dm * 4 + 3 * page * dm * 2 + 3 * n_rows * dm * 4
    return pl.pallas_call(
        functools.partial(_attn_paged_kernel, n_heads=n_heads, dh=dh, dec_seq=dec_seq, lam_init=lam_init),
        grid_spec=grid_spec, out_shape=jax.ShapeDtypeStruct((n_dec * SAMPLE_ROWS, dm), F32),
        compiler_params=_cparams(("arbitrary", "arbitrary"), est), name="attn_paged")(
            page_table.reshape(-1).astype(jnp.int32), q, ck, cv, k, v, lam_vec,
            head_norm.reshape(1, 2 * dh).astype(F32))


def _moe_plan(top_idx, top_prob, n_exp, tm):
    n, k = top_idx.shape
    e_flat = top_idx.reshape(-1)
    order = jnp.argsort(e_flat, stable=True)
    sorted_e = e_flat[order]
    counts = jnp.sum((e_flat[:, None] == jnp.arange(n_exp, dtype=jnp.int32)[None, :]).astype(jnp.int32), axis=0)
    padded = (counts + tm - 1) // tm * tm
    p_end = jnp.cumsum(padded)
    p_start = p_end - padded
    start = jnp.cumsum(counts) - counts
    dest = (p_start[sorted_e] + jnp.arange(n * k, dtype=jnp.int32) - start[sorted_e]).astype(jnp.int32)
    n_tiles_max = (n * k + n_exp * (tm - 1)) // tm
    rows = n_tiles_max * tm
    src_tok = jnp.zeros((rows,), jnp.int32).at[dest].set((order // k).astype(jnp.int32))
    scale = jnp.zeros((rows,), F32).at[dest].set(top_prob.reshape(-1)[order])
    pos = jnp.zeros((n * k,), jnp.int32).at[order].set(dest).reshape(n, k)
    n_tiles = (p_end[-1] // tm).astype(jnp.int32)
    tile_e = jnp.searchsorted(p_end, jnp.arange(n_tiles_max, dtype=jnp.int32) * tm, side="right")
    tile_e = jnp.minimum(tile_e, n_exp - 1).astype(jnp.int32)
    last_e = tile_e[jnp.maximum(n_tiles - 1, 0)]
    tile_e = jnp.where(jnp.arange(n_tiles_max) < n_tiles, tile_e, last_e)
    return src_tok, scale, pos, tile_e, n_tiles.reshape(1)


def _tiles(R):
    tm = _pick(R, (832, 640, 416, 272, 256, 136, 128, 64, 16))
    return tm


def kernel(x_prompt, x_sample, state_gla, state_hgrn, cache_k, cache_v, page_table, meta_tokens, norm_mix, norm_ffn, final_norm, a_w_in, a_gla_w_gate_up, a_gla_b_gate, a_gla_norm, a_hgrn_lb_logits, a_hgrn_norm, a_w_out, c_w_q, c_w_k, c_w_v, c_lambda_q1, c_lambda_k1, c_lambda_q2, c_lambda_k2, c_head_norm, c_w_o, ffn_w_gate, ffn_w_up, ffn_w_down, moe_w_router, moe_w_gate, moe_w_up, moe_w_down):
    B, seq_in, D = x_prompt.shape
    DB, dec_seq, _ = x_sample.shape
    n_meta = meta_tokens.shape[0]
    L = n_meta + seq_in
    BL = B * L
    R = BL + DB * SAMPLE_ROWS
    depth = norm_mix.shape[0]
    GH, gdk, gdv = state_gla.shape[2:]
    HH, hdk, hdv = state_hgrn.shape[2:]
    rank = a_gla_w_gate_up.shape[1]
    n_heads = cache_v.shape[3]
    dh = cache_k.shape[4]
    n_exp = moe_w_router.shape[2]
    past_len = page_table.shape[1] * cache_k.shape[2]
    assert dec_seq <= SAMPLE_ROWS and dh == LANES and BL % SAMPLE_ROWS == 0
    tm = _tiles(R)

    meta = jnp.broadcast_to(meta_tokens[None].astype(F32), (B, n_meta, D))
    xp = jnp.concatenate([meta, x_prompt], axis=1).reshape(BL, D)
    xs = jnp.pad(x_sample, ((0, 0), (0, SAMPLE_ROWS - dec_seq), (0, 0))).reshape(DB * SAMPLE_ROWS, D)
    x = jnp.concatenate([xp, xs], axis=0)

    pos_s = jnp.where(jnp.arange(SAMPLE_ROWS) < dec_seq, past_len + jnp.arange(SAMPLE_ROWS), 0)
    pos = jnp.concatenate([jnp.tile(jnp.arange(L, dtype=jnp.int32), B),
                           jnp.tile(pos_s.astype(jnp.int32), DB)])
    inv = ROPE_THETA ** (-jnp.arange(dh // 2, dtype=F32) * 2.0 / dh)
    ang = pos.astype(F32)[:, None] * inv[None, :]
    rope = (jnp.concatenate([jnp.cos(ang), jnp.cos(ang)], axis=1),
            jnp.concatenate([-jnp.sin(ang), jnp.sin(ang)], axis=1))

    c_low = 2 * GH * gdk + GH * gdv
    deltas = []
    gla_p, gla_s, hgrn_p, hgrn_s, k_p, k_s, v_p, v_s = ([] for _ in range(8))
    for i in range(depth):
        j = i // 2
        if deltas:
            x, xn = _norm(x, deltas, norm_mix[i], want_x=True, out_dtype=BF16)
        else:
            (xn,) = _norm(x, [], norm_mix[i], want_x=False, out_dtype=BF16)
        if i % 2 == 0:
            w_in = a_w_in[j]
            pa = _gmm(xn, [w_in], tm=tm, tn=512 if c_low % 512 == 0 else LANES, n_out=c_low, name="in_a")
            glow = _gmm(xn, [w_in[:, c_low:c_low + rank]], tm=tm, tn=rank, name="in_low")
            w_tail = w_in[:, c_low + rank:]
            pb = _gmm(xn, [w_tail], tm=tm, tn=512 if w_tail.shape[1] % 512 == 0 else LANES, name="in_b")
            wgu = a_gla_w_gate_up[j].astype(F32)
            bg = a_gla_b_gate[j].reshape(1, -1).astype(F32)
            c_hq = GH * gdv
            hg_srcs = [(pb, hdk, c_hq // hdk), (pb, hdk, (c_hq + HH * hdk) // hdk),
                       (pb, hdv, (c_hq + 2 * HH * hdk) // hdv),
                       (a_hgrn_lb_logits.astype(F32), (a_hgrn_lb_logits.shape[0], hdk), lambda b, h: (0, h))]
            hg_og = (pb, hdv, (c_hq + 2 * HH * hdk + HH * hdv) // hdv)
            outs = []
            for (n_b, n_rows, row0, n_valid, s_gla, s_hgrn, odt) in (
                    (B, L, 0, L, None, None, BF16),
                    (DB, SAMPLE_ROWS, BL, dec_seq, jnp.swapaxes(state_gla[j], -1, -2),
                     jnp.swapaxes(state_hgrn[j], -1, -2), F32)):
                rb0 = row0 // n_rows
                gsrc = [(pa, gdk, 0), (pa, gdk, GH), (pa, gdv, 2 * GH * gdk // gdv),
                        (glow, (n_rows, rank), lambda b, h, rb0=rb0: (rb0 + b, 0)),
                        (wgu, (rank, gdk), lambda b, h: (0, h)), (bg, (1, gdk), lambda b, h: (0, h))]
                oa, sa = _recurrence("gla", gsrc, (pb, gdv, 0), a_gla_norm[j], s_gla, n_batch=n_b,
                                     n_heads=GH, n_rows=n_rows, row0=row0, n_valid=n_valid, dk=gdk, dv=gdv,
                                     out_dtype=odt)
                ob, sb = _recurrence("hgrn", hg_srcs, hg_og, a_hgrn_norm[j], s_hgrn, n_batch=n_b,
                                     n_heads=HH, n_rows=n_rows, row0=row0, n_valid=n_valid, dk=hdk, dv=hdv,
                                     out_dtype=odt, slot=j)
                outs.append((oa, ob, jnp.swapaxes(sa, -1, -2), jnp.swapaxes(sb, -1, -2)))
            gla_p.append(outs[0][2]); hgrn_p.append(outs[0][3])
            gla_s.append(outs[1][2]); hgrn_s.append(outs[1][3])
            o = jnp.concatenate([jnp.concatenate([outs[0][0], outs[0][1]], axis=1),
                                 jnp.concatenate([outs[1][0], outs[1][1]], axis=1).astype(BF16)], axis=0)
            mix = _gmm(o, [a_w_out[j]], tm=tm, tn=512 if D % 512 == 0 else LANES, name="w_out")
            x, xn = _norm(x, [mix], norm_ffn[i], want_x=True, out_dtype=BF16)
            dff = ffn_w_gate.shape[2]
            hmid = _gmm(xn, [ffn_w_gate[j], ffn_w_up[j]], tm=tm, tn=256 if dff % 256 == 0 else LANES,
                        epilogue="swiglu", out_dtype=BF16, name="ffn_up")
            kh = dff // 2
            part = _gmm(hmid, [ffn_w_down[j]], tm=tm, tn=256 if D % 256 == 0 else LANES, kb=0, k_blk=kh,
                        name="ffn_down0")
            ffn = _gmm(hmid, [ffn_w_down[j]], tm=tm, tn=256 if D % 256 == 0 else LANES, kb=1, k_blk=kh,
                       partial=part, name="ffn_down1")
            deltas = [ffn]
        else:
            lam_init = 0.8 - 0.6 * math.exp(-0.3 * i)
            tn = 512 if D % 512 == 0 else LANES
            q = _gmm(xn, [c_w_q[j]], tm=tm, tn=tn, epilogue="rope", rope=rope, name="w_q")
            k = _gmm(xn, [c_w_k[j]], tm=tm, tn=tn, epilogue="rope", rope=rope, name="w_k")
            v = _gmm(xn, [c_w_v[j]], tm=tm, tn=tn, name="w_v")
            k_p.append(k[:BL].reshape(B, L, 2 * n_heads, dh))
            v_p.append(v[:BL].reshape(B, L, n_heads, 2 * dh))
            k_s.append(k[BL:].reshape(DB, SAMPLE_ROWS, 2 * n_heads, dh)[:, :dec_seq])
            v_s.append(v[BL:].reshape(DB, SAMPLE_ROWS, n_heads, 2 * dh)[:, :dec_seq])
            lam_vec = jnp.stack([c_lambda_q1[j], c_lambda_k1[j], c_lambda_q2[j], c_lambda_k2[j]]).astype(F32)
            lp = -(-L // LANES) * LANES
            kpad = jnp.pad(k[:BL].reshape(B, L, D), ((0, 0), (0, lp - L), (0, 0)))
            vpad = jnp.pad(v[:BL].reshape(B, L, D), ((0, 0), (0, lp - L), (0, 0)))
            o_p = _attn_prompt(q, kpad, vpad, lam_vec, c_head_norm[j], n_batch=B, seq=L, n_heads=n_heads,
                               dh=dh, lam_init=lam_init)
            o_s = _attn_paged(q, k, v, cache_k, cache_v, page_table, j, lam_vec, c_head_norm[j], row0=BL,
                              n_heads=n_heads, dh=dh, dec_seq=dec_seq, lam_init=lam_init)
            o = jnp.concatenate([o_p, o_s.astype(BF16)], axis=0)
            mix = _gmm(o, [c_w_o[j]], tm=tm, tn=tn, name="w_o")
            x, xn, ti, tp = _norm(x, [mix], norm_ffn[i], want_x=True, out_dtype=BF16, w_router=moe_w_router[j])
            tm_e = MOE_ROW_TILE
            src_tok, scale, pos_e, tile_e, n_tiles = _moe_plan(ti[:, :2].astype(jnp.int32), tp[:, :2], n_exp, tm_e)
            xs_e = jnp.take(xn, src_tok, axis=0)
            eff = moe_w_gate.shape[3]
            h_e = _gmm(xs_e, [moe_w_gate[j], moe_w_up[j]], tm=tm_e, tn=512 if eff % 512 == 0 else LANES,
                       te=tile_e, nv=n_tiles, epilogue="swiglu", out_dtype=BF16, name="moe_up")
            kh = eff // 2
            tn_d = 512 if D % 512 == 0 else LANES
            part = _gmm(h_e, [moe_w_down[j]], tm=tm_e, tn=tn_d, kb=0, k_blk=kh, te=tile_e, nv=n_tiles,
                        name="moe_down0")
            y_e = _gmm(h_e, [moe_w_down[j]], tm=tm_e, tn=tn_d, kb=1, k_blk=kh, te=tile_e, nv=n_tiles,
                       partial=part, scale=scale, name="moe_down1")
            deltas = [jnp.take(y_e, pos_e[:, 0], axis=0), jnp.take(y_e, pos_e[:, 1], axis=0)]
    (y,) = _norm(x, deltas, final_norm, want_x=False, out_dtype=F32)
    y_prompt = y[:BL].reshape(B, L, D)[:, n_meta:]
    y_sample = y[BL:].reshape(DB, SAMPLE_ROWS, D)[:, :dec_seq]
    return (y_prompt, y_sample, jnp.stack(gla_p), jnp.stack(hgrn_p), jnp.stack(k_p), jnp.stack(v_p),
            jnp.stack(gla_s), jnp.stack(hgrn_s), jnp.stack(k_s), jnp.stack(v_s))
```

```python
import functools
import math

import jax
import jax.numpy as jnp
from jax import lax
from jax.experimental import pallas as pl
from jax.experimental.pallas import tpu as pltpu

F32 = jnp.float32
BF16 = jnp.bfloat16
NORM_EPS = 1e-6
ROPE_THETA = 10000.0
GLA_TAU = 16.0
LANES = 128
SAMPLE_ROWS = 8
SUB = 16
MOE_ROW_TILE = 512
VMEM_CAP = 60 * 2 ** 20
VMEM_MIN = 32 * 2 ** 20
NT = (((1,), (1,)), ((), ()))
TN = (((0,), (0,)), ((), ()))


def _cparams(semantics, est_bytes):
    limit = int(min(max(est_bytes * 5 // 4 + (4 << 20), VMEM_MIN), VMEM_CAP))
    return pltpu.CompilerParams(dimension_semantics=semantics, vmem_limit_bytes=limit)


def _pick(n, cands):
    for c in cands:
        if n % c == 0:
            return c
    raise ValueError(f"no tile for {n} in {cands}")


def _split_bf16(x, parts):
    out = []
    for _ in range(parts):
        h = x.astype(BF16)
        out.append(h)
        x = x - h.astype(F32)
    return out


def _dot_f32(a, b):
    a1, a2 = _split_bf16(a, 2)
    b1, b2 = _split_bf16(b, 2)
    d = functools.partial(jnp.dot, preferred_element_type=F32)
    return d(a1, b1) + (d(a1, b2) + d(a2, b1))


def _rms(x, w):
    return x * lax.rsqrt(jnp.mean(x * x, axis=-1, keepdims=True) + NORM_EPS) * w


def _silu(x):
    return x * jax.nn.sigmoid(x)


def _norm_kernel(*refs, n_delta, want_x, n_exp):
    it = iter(refs)
    x_ref = next(it)
    d_refs = [next(it) for _ in range(n_delta)]
    w_ref = next(it)
    wr_ref = next(it) if n_exp else None
    xo_ref = next(it) if want_x else None
    no_ref = next(it)
    x = x_ref[...]
    if n_delta == 1:
        x = x + d_refs[0][...]
    elif n_delta == 2:
        x = x + (d_refs[0][...] + d_refs[1][...])
    if want_x:
        xo_ref[...] = x
    y = _rms(x, w_ref[...])
    no_ref[...] = y.astype(no_ref.dtype)
    if n_exp:
        ti_ref, tp_ref = next(it), next(it)
        logits = _dot_f32(y, wr_ref[...])
        col = lax.broadcasted_iota(jnp.int32, logits.shape, 1)
        colf = col.astype(F32)
        logits = jnp.where(col < n_exp, logits, -jnp.inf)
        m1 = jnp.max(logits, axis=1, keepdims=True)
        i1 = jnp.min(jnp.where(logits == m1, colf, float(LANES)), axis=1, keepdims=True)
        l2 = jnp.where(colf == i1, -jnp.inf, logits)
        m2 = jnp.max(l2, axis=1, keepdims=True)
        i2 = jnp.min(jnp.where(l2 == m2, colf, float(LANES)), axis=1, keepdims=True)
        e = jnp.exp(m2 - m1)
        den = 1.0 + e
        ti_ref[...] = jnp.where(col == 0, i1, jnp.where(col == 1, i2, 0.0))
        tp_ref[...] = jnp.where(col == 0, 1.0 / den, jnp.where(col == 1, e / den, 0.0))


def _norm(x, deltas, w, *, want_x, out_dtype, w_router=None):
    R, D = x.shape
    tr = _pick(R, (208, 160, 144, 136, 128, 64, 48, 32, 16))
    n_exp = 0 if w_router is None else w_router.shape[1]
    row = pl.BlockSpec((tr, D), lambda i: (i, 0))
    args = [x, *deltas, w.reshape(1, D).astype(F32)]
    in_specs = [row] * (1 + len(deltas)) + [pl.BlockSpec((1, D), lambda i: (0, 0))]
    if n_exp:
        args.append(jnp.pad(w_router.astype(F32), ((0, 0), (0, LANES - n_exp))))
        in_specs.append(pl.BlockSpec((D, LANES), lambda i: (0, 0)))
    out_shape, out_specs = [], []
    if want_x:
        out_shape.append(jax.ShapeDtypeStruct((R, D), F32))
        out_specs.append(row)
    out_shape.append(jax.ShapeDtypeStruct((R, D), out_dtype))
    out_specs.append(row)
    if n_exp:
        lane = pl.BlockSpec((tr, LANES), lambda i: (i, 0))
        out_shape += [jax.ShapeDtypeStruct((R, LANES), F32)] * 2
        out_specs += [lane, lane]
    est = 2 * tr * D * 4 * (3 + len(deltas)) + (D * LANES * 8 if n_exp else 0) + 4 * tr * D * 4
    return pl.pallas_call(
        functools.partial(_norm_kernel, n_delta=len(deltas), want_x=want_x, n_exp=n_exp),
        grid=(R // tr,), in_specs=in_specs, out_specs=out_specs, out_shape=out_shape,
        compiler_params=_cparams(("arbitrary",), est), name="norm")(*args)


def _gmm_kernel(te_ref, nv_ref, tv_ref, *refs, n_w, epilogue, has_partial, has_scale, half_rows):
    it = iter(refs)
    a_ref = next(it)
    w_refs = [next(it) for _ in range(n_w)]
    p_ref = next(it) if has_partial else None
    s_ref = next(it) if has_scale else None
    cos_ref, sin_ref = (next(it), next(it)) if epilogue == "rope" else (None, None)
    o_ref = next(it)
    wbf = [next(it) for _ in range(n_w)]
    tm = a_ref.shape[0]
    i = pl.program_id(1)
    is_new = jnp.logical_or(i == 0, te_ref[i] != te_ref[jnp.maximum(i - 1, 0)])

    @pl.when(is_new)
    def _():
        for w_ref, s in zip(w_refs, wbf):
            s[...] = w_ref[0].astype(BF16)

    def compute(rows):
        rs = slice(0, rows)
        a = a_ref[rs, :]
        acc = jnp.dot(a, wbf[0][...], preferred_element_type=F32)
        if epilogue == "swiglu":
            acc = _silu(acc) * jnp.dot(a, wbf[1][...], preferred_element_type=F32)
        if has_partial:
            acc = p_ref[rs, :] + acc
        if has_scale:
            acc = acc * s_ref[rs, :]
        if epilogue == "rope":
            cos, sin = cos_ref[rs, :], sin_ref[rs, :]
            for g in range(acc.shape[1] // LANES):
                t = acc[:, g * LANES:(g + 1) * LANES]
                o_ref[rs, g * LANES:(g + 1) * LANES] = (
                    t * cos + pltpu.roll(t, LANES // 2, 1) * sin).astype(o_ref.dtype)
        else:
            o_ref[rs, :] = acc.astype(o_ref.dtype)
        if rows < tm:
            o_ref[rows:, :] = jnp.zeros((tm - rows, o_ref.shape[1]), o_ref.dtype)

    live = i < nv_ref[0]
    if half_rows:
        full = tv_ref[i] > tm // 2
        pl.when(jnp.logical_and(live, full))(functools.partial(compute, tm))
        pl.when(jnp.logical_and(live, jnp.logical_not(full)))(functools.partial(compute, tm // 2))
    else:
        pl.when(live)(functools.partial(compute, tm))

    @pl.when(jnp.logical_not(live))
    def _():
        o_ref[...] = jnp.zeros_like(o_ref)


def _gmm(a, ws, *, tm, tn, n_out=None, w_col0=0, kb=0, k_blk=None, te=None, nv=None, tv=None,
         epilogue="plain", out_dtype=F32, partial=None, scale=None, rope=None, name="gmm"):
    R = a.shape[0]
    ws = [w if w.ndim == 3 else w[None] for w in ws]
    k_blk = a.shape[1] if k_blk is None else k_blk
    n_out = ws[0].shape[2] if n_out is None else n_out
    assert R % tm == 0 and n_out % tn == 0 and w_col0 % tn == 0
    T = R // tm
    half_rows = tv is not None
    if te is None:
        te = jnp.zeros((T,), jnp.int32)
        nv = jnp.full((1,), T, jnp.int32)
    if tv is None:
        tv = jnp.full((T,), tm, jnp.int32)
    c0 = w_col0 // tn

    def a_map(j, i, te_r, nv_r, tv_r):
        return (jnp.minimum(i, nv_r[0] - 1), kb)

    def w_map(j, i, te_r, nv_r, tv_r):
        return (te_r[i], kb, j + c0)

    def o_map(j, i, te_r, nv_r, tv_r):
        return (i, j)

    def r_map(j, i, te_r, nv_r, tv_r):
        return (i, 0)

    args = [a, *ws]
    in_specs = [pl.BlockSpec((tm, k_blk), a_map)] + [pl.BlockSpec((1, k_blk, tn), w_map)] * len(ws)
    est = 2 * tm * k_blk * a.dtype.itemsize + len(ws) * k_blk * max(tn, LANES) * (2 * 4 + 2)
    est += tm * tn * (2 * jnp.dtype(out_dtype).itemsize + 4 * (2 + len(ws)))
    if partial is not None:
        args.append(partial)
        in_specs.append(pl.BlockSpec((tm, tn), o_map))
        est += 2 * tm * tn * 4
    if scale is not None:
        args.append(scale.reshape(R, 1))
        in_specs.append(pl.BlockSpec((tm, 1), r_map))
        est += 2 * tm * LANES * 4
    if rope is not None:
        args += [rope[0], rope[1]]
        in_specs += [pl.BlockSpec((tm, LANES), r_map)] * 2
        est += 4 * tm * LANES * 4
    grid_spec = pltpu.PrefetchScalarGridSpec(
        num_scalar_prefetch=3, grid=(n_out // tn, T), in_specs=in_specs,
        out_specs=pl.BlockSpec((tm, tn), o_map),
        scratch_shapes=[pltpu.VMEM((k_blk, tn), BF16) for _ in ws])
    return pl.pallas_call(
        functools.partial(_gmm_kernel, n_w=len(ws), epilogue=epilogue,
                          has_partial=partial is not None, has_scale=scale is not None, half_rows=half_rows),
        grid_spec=grid_spec, out_shape=jax.ShapeDtypeStruct((R, n_out), out_dtype),
        compiler_params=_cparams(("arbitrary", "arbitrary"), est), name=name)(te, nv, tv, *args)


def _rec_block(q, k, v, g, st_ref, hh, sub):
    T, dk = q.shape
    rr = lax.broadcasted_iota(jnp.int32, (T, T), 0)
    cc = lax.broadcasted_iota(jnp.int32, (T, T), 1)
    tril = jnp.where(cc <= rr, 1.0, 0.0).astype(BF16)
    cum = sum(jnp.dot(tril, part, preferred_element_type=F32) for part in _split_bf16(g, 3))
    last = cum[T - 1:T, :]
    st = st_ref[0, hh]
    o_inter = lax.dot_general((q * jnp.exp(cum)).astype(BF16), st.astype(BF16), NT,
                              preferred_element_type=F32)
    k_end = (k * jnp.exp(last - cum)).astype(BF16)
    vb = v.astype(BF16)
    st_ref[0, hh] = st * jnp.exp(last) + lax.dot_general(vb, k_end, TN, preferred_element_type=F32)
    row_in_sub = lax.broadcasted_iota(jnp.int32, (sub, dk), 0)
    lane_in_sub = lax.broadcasted_iota(jnp.int32, (sub, sub), 1)
    outs = []
    for a in range(T // sub):
        lo, hi = a * sub, (a + 1) * sub
        qa, ka, ca = q[lo:hi], k[lo:hi], cum[lo:hi]
        att = jnp.zeros((sub, sub), F32)
        for j in range(sub):
            decay = jnp.where(row_in_sub >= j, jnp.exp(ca - ca[j:j + 1, :]), 0.0)
            col = jnp.sum(qa * ka[j:j + 1, :] * decay, axis=1, keepdims=True)
            att = jnp.where(lane_in_sub == j, col, att)
        oa = o_inter[lo:hi] + jnp.dot(att.astype(BF16), vb[lo:hi], preferred_element_type=F32)
        if a > 0:
            cb = cum[lo - 1:lo, :]
            qt = (qa * jnp.exp(ca - cb)).astype(BF16)
            kt = (k[:lo] * jnp.exp(cb - cum[:lo])).astype(BF16)
            att_prev = lax.dot_general(qt, kt, NT, preferred_element_type=F32)
            oa = oa + jnp.dot(att_prev.astype(BF16), vb[:lo], preferred_element_type=F32)
        outs.append(oa)
    return outs[0] if len(outs) == 1 else jnp.concatenate(outs, axis=0)


def _rec_kernel(*refs, mode, n_rows, blk, n_valid, has_init, slot, hp, dk, dv):
    it = iter(refs)
    if mode == "gla":
        q_ref, k_ref, v_ref, gl_ref, wg_ref, bg_ref = (next(it) for _ in range(6))
    else:
        q_ref, f_ref, v_ref, lb_ref = (next(it) for _ in range(4))
    og_ref, nw_ref = next(it), next(it)
    s0_ref = next(it) if has_init else None
    o_ref, st_ref = next(it), next(it)
    if has_init:
        st_ref[...] = s0_ref[...]
    else:
        st_ref[...] = jnp.zeros_like(st_ref)
    pad = blk - n_rows if n_rows < blk else 0

    def load(ref, r0, hh, width):
        cols = slice(hh * width, (hh + 1) * width)
        if pad:
            x = ref[:, cols]
            return jnp.concatenate([x, jnp.zeros((pad, width), x.dtype)], axis=0)
        return ref[pl.ds(r0, blk), cols]

    def body(ib, carry):
        r0 = pl.multiple_of(ib * blk, SUB)
        for hh in range(hp):
            kc = slice(hh * dk, (hh + 1) * dk)
            v = load(v_ref, r0, hh, dv)
            if mode == "gla":
                q = load(q_ref, r0, hh, dk) * dk ** -0.5
                k = load(k_ref, r0, hh, dk)
                z = _dot_f32(load(gl_ref, r0, 0, gl_ref.shape[1]), wg_ref[:, kc]) + bg_ref[:, kc]
                g = (jnp.minimum(z, 0.0) - jnp.log1p(jnp.exp(-jnp.abs(z)))) / GLA_TAU
            else:
                lg = lb_ref[:, kc]
                ex = jnp.exp(lg - jnp.max(lg, axis=0, keepdims=True))
                sm = ex / jnp.sum(ex, axis=0, keepdims=True)
                lb = jnp.sum(sm[:slot + 1], axis=0, keepdims=True)
                f = lb + (1.0 - lb) * jax.nn.sigmoid(load(f_ref, r0, hh, dk))
                q = _silu(load(q_ref, r0, hh, dk))
                k = 1.0 - f
                g = jnp.log(f)
            if n_valid < blk:
                live = lax.broadcasted_iota(jnp.int32, (blk, dk), 0) < n_valid
                k = jnp.where(live, k, 0.0)
                g = jnp.where(live, g, 0.0)
            o = _rec_block(q, k, v, g, st_ref, hh, SUB)
            y = _rms(o, nw_ref[...]) * _silu(load(og_ref, r0, hh, dv))
            vc = slice(hh * dv, (hh + 1) * dv)
            if pad:
                o_ref[:, vc] = y[:n_rows].astype(o_ref.dtype)
            else:
                o_ref[pl.ds(r0, blk), vc] = y.astype(o_ref.dtype)
        return carry

    n_blk = 1 if pad else n_rows // blk
    if n_blk == 1:
        body(0, 0)
    else:
        lax.fori_loop(0, n_blk, body, 0)


def _recurrence(mode, srcs, norm_w, s0t, *, n_batch, n_heads, hp, n_rows, row0, n_valid, dk, dv,
                out_dtype, slot=0):
    rb0 = row0 // n_rows
    assert row0 % n_rows == 0 and n_heads % hp == 0
    blk = SUB if n_rows < SUB else _pick(n_rows, (64, 48, 32, 16))
    args, in_specs = [], []
    est = 0
    for src in srcs:
        args.append(src[1])
        if src[0] == "head":
            _, arr, width, cb0 = src
            assert cb0 % hp == 0
            in_specs.append(pl.BlockSpec((n_rows, width * hp), lambda b, h, c=cb0 // hp: (rb0 + b, c + h)))
            est += 2 * n_rows * width * hp * 4
        elif src[0] == "rows":
            in_specs.append(pl.BlockSpec((n_rows, src[1].shape[1]), lambda b, h: (rb0 + b, 0)))
            est += 2 * n_rows * LANES * 4
        else:
            _, arr, width = src
            in_specs.append(pl.BlockSpec((arr.shape[0], width * hp), lambda b, h: (0, h)))
    args.append(norm_w.reshape(1, dv).astype(F32))
    in_specs.append(pl.BlockSpec((1, dv), lambda b, h: (0, 0)))
    st_spec = pl.BlockSpec((1, hp, dv, dk), lambda b, h: (b, h, 0, 0))
    if s0t is not None:
        args.append(s0t)
        in_specs.append(st_spec)
    est += 4 * hp * dv * dk * 4 + 2 * n_rows * hp * dv * 4 + 64 * blk * max(dk, dv) * 4
    return pl.pallas_call(
        functools.partial(_rec_kernel, mode=mode, n_rows=n_rows, blk=blk, n_valid=n_valid,
                          has_init=s0t is not None, slot=slot, hp=hp, dk=dk, dv=dv),
        grid=(n_batch, n_heads // hp), in_specs=in_specs,
        out_specs=[pl.BlockSpec((n_rows, hp * dv), lambda b, h: (b, h)), st_spec],
        out_shape=[jax.ShapeDtypeStruct((n_batch * n_rows, n_heads * dv), out_dtype),
                   jax.ShapeDtypeStruct((n_batch, n_heads, dv, dk), F32)],
        compiler_params=_cparams(("arbitrary", "arbitrary"), est), name="rec_" + mode)(*args)


def _lambda(lam_ref, lam_init):
    lv = lam_ref[...]
    return (jnp.exp(jnp.sum(lv[0:1] * lv[1:2], axis=1, keepdims=True))
            - jnp.exp(jnp.sum(lv[2:3] * lv[3:4], axis=1, keepdims=True)) + lam_init)


def _attn_prompt_kernel(q_ref, k_ref, v_ref, lam_ref, hn_ref, o_ref, *, lam_init, n_q):
    tq, dh2 = q_ref.shape
    dh = dh2 // 2
    lp = k_ref.shape[1]
    lam = _lambda(lam_ref, lam_init)
    q = q_ref[...]
    qi = pl.program_id(2)

    def attend(n_keys):
        rows = qi * tq + lax.broadcasted_iota(jnp.int32, (tq, n_keys), 0)
        causal = lax.broadcasted_iota(jnp.int32, (tq, n_keys), 1) <= rows
        probs = []
        for r in range(2):
            kr = k_ref[0, 0:n_keys, r * dh:(r + 1) * dh].astype(BF16)
            s = lax.dot_general(q[:, r * dh:(r + 1) * dh].astype(BF16), kr, NT,
                                preferred_element_type=F32) * dh ** -0.5
            s = jnp.where(causal, s, -jnp.inf)
            e = jnp.exp(s - jnp.max(s, axis=1, keepdims=True))
            probs.append(e * (1.0 / jnp.sum(e, axis=1, keepdims=True)))
        a = probs[0] - lam * probs[1]
        o = jnp.dot(a.astype(BF16), v_ref[0, 0:n_keys, :].astype(BF16), preferred_element_type=F32)
        o_ref[...] = (_rms(o, hn_ref[...]) * (1.0 - lam_init)).astype(o_ref.dtype)

    for i in range(n_q):
        n_keys = min(lp, -(-(i + 1) * tq // LANES) * LANES)
        pl.when(qi == i)(functools.partial(attend, n_keys))


def _attn_prompt(q, kpad, vpad, lam_vec, head_norm, *, n_batch, seq, n_heads, dh, lam_init):
    lp = kpad.shape[1]
    tq = _pick(seq, (344, 256, 128, 64, 16))
    nq = seq // tq
    est = 4 * lp * 2 * dh * 4 + 4 * tq * 2 * dh * 4 + 8 * tq * lp * 4
    return pl.pallas_call(
        functools.partial(_attn_prompt_kernel, lam_init=lam_init, n_q=nq),
        grid=(n_batch, n_heads, nq),
        in_specs=[pl.BlockSpec((tq, 2 * dh), lambda b, h, i: (b * nq + i, h)),
                  pl.BlockSpec((1, lp, 2 * dh), lambda b, h, i: (b, 0, h)),
                  pl.BlockSpec((1, lp, 2 * dh), lambda b, h, i: (b, 0, h)),
                  pl.BlockSpec((4, dh), lambda b, h, i: (0, 0)),
                  pl.BlockSpec((1, 2 * dh), lambda b, h, i: (0, 0))],
        out_specs=pl.BlockSpec((tq, 2 * dh), lambda b, h, i: (b * nq + i, h)),
        out_shape=jax.ShapeDtypeStruct((n_batch * seq, n_heads * 2 * dh), BF16),
        compiler_params=_cparams(("arbitrary",) * 3, est), name="attn_prompt")(
            q, kpad, vpad, lam_vec, head_norm.reshape(1, 2 * dh).astype(F32))


def _attn_paged_kernel(pt_ref, q_ref, kp_ref, vlo_ref, vhi_ref, kn_ref, vn_ref, lam_ref, hn_ref, o_ref,
                       qbd, m_s, d_s, acc, kflat, vflat, *, n_heads, dh, dec_seq, lam_init):
    p = pl.program_id(1)
    n_rows = qbd.shape[0]
    grp = 2 * SAMPLE_ROWS

    @pl.when(p == 0)
    def _():
        q8 = q_ref[...] * dh ** -0.5
        qt = jnp.concatenate([q8] * (2 * n_heads), axis=0)
        rg = lax.broadcasted_iota(jnp.int32, qt.shape, 0) >> (SAMPLE_ROWS.bit_length() - 1)
        cg = lax.broadcasted_iota(jnp.int32, qt.shape, 1) >> (dh.bit_length() - 1)
        qbd[...] = jnp.where(rg == cg, qt, 0.0).astype(BF16)
        m_s[...] = jnp.full_like(m_s, -jnp.inf)
        d_s[...] = jnp.zeros_like(d_s)
        acc[...] = jnp.zeros_like(acc)

    def merge(kt, vt, valid):
        s = lax.dot_general(qbd[...], kt.astype(BF16), NT, preferred_element_type=F32)
        if valid is not None:
            s = jnp.where(valid, s, -jnp.inf)
        m_old = m_s[...]
        m_new = jnp.maximum(m_old, jnp.max(s, axis=1, keepdims=True))
        corr = jnp.exp(m_old - m_new)
        pe = jnp.exp(s - m_new)
        d_s[...] = d_s[...] * corr + jnp.sum(pe, axis=1, keepdims=True)
        m_s[...] = m_new
        res = jnp.dot(pe.astype(BF16), vt.astype(BF16), preferred_element_type=F32)
        for h in range(n_heads):
            r = slice(h * grp, (h + 1) * grp)
            acc[r, :] = acc[r, :] * corr[r] + res[r, h * 2 * dh:(h + 1) * 2 * dh]

    page = kflat.shape[0]
    for g in range(2 * n_heads):
        kflat[:, g * dh:(g + 1) * dh] = kp_ref[0, pl.ds(g, page, stride=2 * n_heads), :].astype(BF16)
    for h in range(n_heads):
        rows = pl.ds(h, page, stride=n_heads)
        vflat[:, 2 * h * dh:(2 * h + 1) * dh] = vlo_ref[0, rows, :].astype(BF16)
        vflat[:, (2 * h + 1) * dh:(2 * h + 2) * dh] = vhi_ref[0, rows, :].astype(BF16)
    merge(kflat[...], vflat[...], None)

    @pl.when(p == pl.num_programs(1) - 1)
    def _():
        zeros = jnp.zeros_like(kn_ref[...])
        kn = jnp.concatenate([kn_ref[...], zeros], axis=0)
        vn = jnp.concatenate([vn_ref[...], zeros], axis=0)
        tok = lax.broadcasted_iota(jnp.int32, (n_rows, 2 * SAMPLE_ROWS), 0) & (SAMPLE_ROWS - 1)
        key = lax.broadcasted_iota(jnp.int32, (n_rows, 2 * SAMPLE_ROWS), 1)
        merge(kn, vn, jnp.logical_and(key <= tok, key < dec_seq))
        lam = _lambda(lam_ref, lam_init)
        o = acc[...] / d_s[...]
        for h in range(n_heads):
            oh = o[h * grp:h * grp + SAMPLE_ROWS] - lam * o[h * grp + SAMPLE_ROWS:(h + 1) * grp]
            o_ref[:, h * 2 * dh:(h + 1) * 2 * dh] = _rms(oh, hn_ref[...]) * (1.0 - lam_init)


def _attn_paged(q, k, v, cache_k, cache_v, page_table, layer, lam_vec, head_norm, *, row0, n_heads, dh,
                dec_seq, lam_init):
    n_dec, n_pages = page_table.shape
    n_pool, page = cache_k.shape[1], cache_k.shape[2]
    dm = n_heads * 2 * dh
    ck = cache_k.reshape(-1, page * 2 * n_heads, dh)
    cv = cache_v.reshape(-1, page * n_heads, 2 * dh)
    rb0 = row0 // SAMPLE_ROWS
    n_rows = 2 * n_heads * SAMPLE_ROWS
    base = layer * n_pool

    def page_map(b, p, pt):
        return (base + pt[b * n_pages + p], 0, 0)

    def new_map(b, p, pt):
        return (rb0 + b, 0)

    grid_spec = pltpu.PrefetchScalarGridSpec(
        num_scalar_prefetch=1, grid=(n_dec, n_pages),
        in_specs=[pl.BlockSpec((SAMPLE_ROWS, dm), new_map),
                  pl.BlockSpec((1, page * 2 * n_heads, dh), page_map),
                  pl.BlockSpec((1, page * n_heads, dh), page_map),
                  pl.BlockSpec((1, page * n_heads, dh), lambda b, p, pt: page_map(b, p, pt)[:2] + (1,)),
                  pl.BlockSpec((SAMPLE_ROWS, dm), new_map), pl.BlockSpec((SAMPLE_ROWS, dm), new_map),
                  pl.BlockSpec((4, dh), lambda b, p, pt: (0, 0)),
                  pl.BlockSpec((1, 2 * dh), lambda b, p, pt: (0, 0))],
        out_specs=pl.BlockSpec((SAMPLE_ROWS, dm), lambda b, p, pt: (b, 0)),
        scratch_shapes=[pltpu.VMEM((n_rows, dm), BF16), pltpu.VMEM((n_rows, 1), F32),
                        pltpu.VMEM((n_rows, 1), F32), pltpu.VMEM((n_rows, 2 * dh), F32),
                        pltpu.VMEM((page, dm), BF16), pltpu.VMEM((page, dm), BF16)])
    est = 4 * page * dm * 4 + 3 * page * dm * 2 + 3 * n_rows * dm * 4
    return pl.pallas_call(
        functools.partial(_attn_paged_kernel, n_heads=n_heads, dh=dh, dec_seq=dec_seq, lam_init=lam_init),
        grid_spec=grid_spec, out_shape=jax.ShapeDtypeStruct((n_dec * SAMPLE_ROWS, dm), F32),
        compiler_params=_cparams(("arbitrary", "arbitrary"), est), name="attn_paged")(
            page_table.reshape(-1).astype(jnp.int32), q, ck, cv, cv, k, v, lam_vec,
            head_norm.reshape(1, 2 * dh).astype(F32))


def _moe_plan(top_idx, top_prob, n_exp, tm):
    n, k = top_idx.shape
    e_flat = top_idx.reshape(-1)
    order = jnp.argsort(e_flat, stable=True)
    sorted_e = e_flat[order]
    counts = jnp.sum((e_flat[:, None] == jnp.arange(n_exp, dtype=jnp.int32)[None, :]).astype(jnp.int32), axis=0)
    padded = (counts + tm - 1) // tm * tm
    p_end = jnp.cumsum(padded)
    p_start = p_end - padded
    start = jnp.cumsum(counts) - counts
    dest = (p_start[sorted_e] + jnp.arange(n * k, dtype=jnp.int32) - start[sorted_e]).astype(jnp.int32)
    n_tiles_max = (n * k + n_exp * (tm - 1)) // tm
    rows = n_tiles_max * tm
    src_tok = jnp.zeros((rows,), jnp.int32).at[dest].set((order // k).astype(jnp.int32))
    scale = jnp.zeros((rows,), F32).at[dest].set(top_prob.reshape(-1)[order])
    pos = jnp.zeros((n * k,), jnp.int32).at[order].set(dest).reshape(n, k)
    n_tiles = (p_end[-1] // tm).astype(jnp.int32)
    tile_e = jnp.searchsorted(p_end, jnp.arange(n_tiles_max, dtype=jnp.int32) * tm, side="right")
    tile_e = jnp.minimum(tile_e, n_exp - 1).astype(jnp.int32)
    last_e = tile_e[jnp.maximum(n_tiles - 1, 0)]
    tile_e = jnp.where(jnp.arange(n_tiles_max) < n_tiles, tile_e, last_e)
    tile_rows = jnp.clip(p_start[tile_e] + counts[tile_e] - jnp.arange(n_tiles_max, dtype=jnp.int32) * tm, 0, tm)
    return src_tok, scale, pos, tile_e, n_tiles.reshape(1), tile_rows.astype(jnp.int32)


def _tiles(R):
    tm = _pick(R, (832, 640, 416, 272, 256, 136, 128, 64, 16))
    return tm


def kernel(x_prompt, x_sample, state_gla, state_hgrn, cache_k, cache_v, page_table, meta_tokens, norm_mix, norm_ffn, final_norm, a_w_in, a_gla_w_gate_up, a_gla_b_gate, a_gla_norm, a_hgrn_lb_logits, a_hgrn_norm, a_w_out, c_w_q, c_w_k, c_w_v, c_lambda_q1, c_lambda_k1, c_lambda_q2, c_lambda_k2, c_head_norm, c_w_o, ffn_w_gate, ffn_w_up, ffn_w_down, moe_w_router, moe_w_gate, moe_w_up, moe_w_down):
    B, seq_in, D = x_prompt.shape
    DB, dec_seq, _ = x_sample.shape
    n_meta = meta_tokens.shape[0]
    L = n_meta + seq_in
    BL = B * L
    R = BL + DB * SAMPLE_ROWS
    depth = norm_mix.shape[0]
    GH, gdk, gdv = state_gla.shape[2:]
    HH, hdk, hdv = state_hgrn.shape[2:]
    rank = a_gla_w_gate_up.shape[1]
    n_heads = cache_v.shape[3]
    dh = cache_k.shape[4]
    n_exp = moe_w_router.shape[2]
    past_len = page_table.shape[1] * cache_k.shape[2]
    assert dec_seq <= SAMPLE_ROWS and dh == LANES and BL % SAMPLE_ROWS == 0
    tm = _tiles(R)

    meta = jnp.broadcast_to(meta_tokens[None].astype(F32), (B, n_meta, D))
    xp = jnp.concatenate([meta, x_prompt], axis=1).reshape(BL, D)
    xs = jnp.pad(x_sample, ((0, 0), (0, SAMPLE_ROWS - dec_seq), (0, 0))).reshape(DB * SAMPLE_ROWS, D)
    x = jnp.concatenate([xp, xs], axis=0)

    pos_s = jnp.where(jnp.arange(SAMPLE_ROWS) < dec_seq, past_len + jnp.arange(SAMPLE_ROWS), 0)
    pos = jnp.concatenate([jnp.tile(jnp.arange(L, dtype=jnp.int32), B),
                           jnp.tile(pos_s.astype(jnp.int32), DB)])
    inv = ROPE_THETA ** (-jnp.arange(dh // 2, dtype=F32) * 2.0 / dh)
    ang = pos.astype(F32)[:, None] * inv[None, :]
    rope = (jnp.concatenate([jnp.cos(ang), jnp.cos(ang)], axis=1),
            jnp.concatenate([-jnp.sin(ang), jnp.sin(ang)], axis=1))

    c_low = 2 * GH * gdk + GH * gdv
    deltas = []
    gla_p, gla_s, hgrn_p, hgrn_s, k_p, k_s, v_p, v_s = ([] for _ in range(8))
    for i in range(depth):
        j = i // 2
        if deltas:
            x, xn = _norm(x, deltas, norm_mix[i], want_x=True, out_dtype=BF16)
        else:
            (xn,) = _norm(x, [], norm_mix[i], want_x=False, out_dtype=BF16)
        if i % 2 == 0:
            w_in = a_w_in[j]
            pa = _gmm(xn, [w_in], tm=tm, tn=512 if c_low % 512 == 0 else LANES, n_out=c_low, name="in_a")
            glow = _gmm(xn, [w_in[:, c_low:c_low + rank]], tm=tm, tn=rank, name="in_low")
            w_tail = w_in[:, c_low + rank:]
            pb = _gmm(xn, [w_tail], tm=tm, tn=512 if w_tail.shape[1] % 512 == 0 else LANES, name="in_b")
            wgu = a_gla_w_gate_up[j].astype(F32)
            bg = a_gla_b_gate[j].reshape(1, -1).astype(F32)
            c_hq = GH * gdv
            c_hf, c_hi, c_ho = c_hq + HH * hdk, c_hq + 2 * HH * hdk, c_hq + 2 * HH * hdk + HH * hdv
            hp_h = 4 if HH % 4 == 0 else 1
            outs = []
            for (n_b, n_rows, row0, n_valid, s_gla, s_hgrn, odt) in (
                    (B, L, 0, L, None, None, BF16),
                    (DB, SAMPLE_ROWS, BL, dec_seq, jnp.swapaxes(state_gla[j], -1, -2),
                     jnp.swapaxes(state_hgrn[j], -1, -2), F32)):
                gsrc = [("head", pa, gdk, 0), ("head", pa, gdk, GH), ("head", pa, gdv, 2 * GH * gdk // gdv),
                        ("rows", glow), ("cols", wgu, gdk), ("cols", bg, gdk), ("head", pb, gdv, 0)]
                oa, sa = _recurrence("gla", gsrc, a_gla_norm[j], s_gla, n_batch=n_b, n_heads=GH, hp=1,
                                     n_rows=n_rows, row0=row0, n_valid=n_valid, dk=gdk, dv=gdv, out_dtype=odt)
                hsrc = [("head", pb, hdk, c_hq // hdk), ("head", pb, hdk, c_hf // hdk),
                        ("head", pb, hdv, c_hi // hdv), ("cols", a_hgrn_lb_logits.astype(F32), hdk),
                        ("head", pb, hdv, c_ho // hdv)]
                ob, sb = _recurrence("hgrn", hsrc, a_hgrn_norm[j], s_hgrn, n_batch=n_b, n_heads=HH, hp=hp_h,
                                     n_rows=n_rows, row0=row0, n_valid=n_valid, dk=hdk, dv=hdv, out_dtype=odt,
                                     slot=j)
                outs.append((oa, ob, jnp.swapaxes(sa, -1, -2), jnp.swapaxes(sb, -1, -2)))
            gla_p.append(outs[0][2]); hgrn_p.append(outs[0][3])
            gla_s.append(outs[1][2]); hgrn_s.append(outs[1][3])
            o = jnp.concatenate([jnp.concatenate([outs[0][0], outs[0][1]], axis=1),
                                 jnp.concatenate([outs[1][0], outs[1][1]], axis=1).astype(BF16)], axis=0)
            mix = _gmm(o, [a_w_out[j]], tm=tm, tn=512 if D % 512 == 0 else LANES, name="w_out")
            x, xn = _norm(x, [mix], norm_ffn[i], want_x=True, out_dtype=BF16)
            dff = ffn_w_gate.shape[2]
            hmid = _gmm(xn, [ffn_w_gate[j], ffn_w_up[j]], tm=tm, tn=256 if dff % 256 == 0 else LANES,
                        epilogue="swiglu", out_dtype=BF16, name="ffn_up")
            kh = dff // 2
            part = _gmm(hmid, [ffn_w_down[j]], tm=tm, tn=256 if D % 256 == 0 else LANES, kb=0, k_blk=kh,
                        name="ffn_down0")
            ffn = _gmm(hmid, [ffn_w_down[j]], tm=tm, tn=256 if D % 256 == 0 else LANES, kb=1, k_blk=kh,
                       partial=part, name="ffn_down1")
            deltas = [ffn]
        else:
            lam_init = 0.8 - 0.6 * math.exp(-0.3 * i)
            tn = 512 if D % 512 == 0 else LANES
            q = _gmm(xn, [c_w_q[j]], tm=tm, tn=tn, epilogue="rope", rope=rope, name="w_q")
            k = _gmm(xn, [c_w_k[j]], tm=tm, tn=tn, epilogue="rope", rope=rope, name="w_k")
            v = _gmm(xn, [c_w_v[j]], tm=tm, tn=tn, name="w_v")
            k_p.append(k[:BL].reshape(B, L, 2 * n_heads, dh))
            v_p.append(v[:BL].reshape(B, L, n_heads, 2 * dh))
            k_s.append(k[BL:].reshape(DB, SAMPLE_ROWS, 2 * n_heads, dh)[:, :dec_seq])
            v_s.append(v[BL:].reshape(DB, SAMPLE_ROWS, n_heads, 2 * dh)[:, :dec_seq])
            lam_vec = jnp.stack([c_lambda_q1[j], c_lambda_k1[j], c_lambda_q2[j], c_lambda_k2[j]]).astype(F32)
            lp = -(-L // LANES) * LANES
            kpad = jnp.pad(k[:BL].reshape(B, L, D), ((0, 0), (0, lp - L), (0, 0)))
            vpad = jnp.pad(v[:BL].reshape(B, L, D), ((0, 0), (0, lp - L), (0, 0)))
            o_p = _attn_prompt(q, kpad, vpad, lam_vec, c_head_norm[j], n_batch=B, seq=L, n_heads=n_heads,
                               dh=dh, lam_init=lam_init)
            o_s = _attn_paged(q, k, v, cache_k, cache_v, page_table, j, lam_vec, c_head_norm[j], row0=BL,
                              n_heads=n_heads, dh=dh, dec_seq=dec_seq, lam_init=lam_init)
            o = jnp.concatenate([o_p, o_s.astype(BF16)], axis=0)
            mix = _gmm(o, [c_w_o[j]], tm=tm, tn=tn, name="w_o")
            x, xn, ti, tp = _norm(x, [mix], norm_ffn[i], want_x=True, out_dtype=BF16, w_router=moe_w_router[j])
            tm_e = MOE_ROW_TILE
            src_tok, scale, pos_e, tile_e, n_tiles, tile_rows = _moe_plan(
                ti[:, :2].astype(jnp.int32), tp[:, :2], n_exp, tm_e)
            xs_e = jnp.take(xn, src_tok, axis=0)
            eff = moe_w_gate.shape[3]
            routed = dict(tm=tm_e, te=tile_e, nv=n_tiles, tv=tile_rows)
            h_e = _gmm(xs_e, [moe_w_gate[j], moe_w_up[j]], tn=512 if eff % 512 == 0 else LANES,
                       epilogue="swiglu", out_dtype=BF16, name="moe_up", **routed)
            kh = eff // 2
            tn_d = 256 if D % 256 == 0 else LANES
            part = _gmm(h_e, [moe_w_down[j]], tn=tn_d, kb=0, k_blk=kh, name="moe_down0", **routed)
            y_e = _gmm(h_e, [moe_w_down[j]], tn=tn_d, kb=1, k_blk=kh, partial=part, scale=scale,
                       name="moe_down1", **routed)
            deltas = [jnp.take(y_e, pos_e[:, 0], axis=0), jnp.take(y_e, pos_e[:, 1], axis=0)]
    (y,) = _norm(x, deltas, final_norm, want_x=False, out_dtype=F32)
    y_prompt = y[:BL].reshape(B, L, D)[:, n_meta:]
    y_sample = y[BL:].reshape(DB, SAMPLE_ROWS, D)[:, :dec_seq]
    return (y_prompt, y_sample, jnp.stack(gla_p), jnp.stack(hgrn_p), jnp.stack(k_p), jnp.stack(v_p),
            jnp.stack(gla_s), jnp.stack(hgrn_s), jnp.stack(k_s), jnp.stack(v_s))
```

```python
import functools
import math

import jax
import jax.numpy as jnp
from jax import lax
from jax.experimental import pallas as pl
from jax.experimental.pallas import tpu as pltpu

F32 = jnp.float32
BF16 = jnp.bfloat16
NORM_EPS = 1e-6
ROPE_THETA = 10000.0
GLA_TAU = 16.0
LANES = 128
SAMPLE_ROWS = 8
HEAD_GROUP = 8
PAGES_PER_STEP = 2
SUB = 16
MOE_ROW_TILE = 512
VMEM_CAP = 60 * 2 ** 20
VMEM_MIN = 32 * 2 ** 20
NT = (((1,), (1,)), ((), ()))
TN = (((0,), (0,)), ((), ()))


def _cparams(semantics, est_bytes):
    limit = int(min(max(est_bytes * 5 // 4 + (4 << 20), VMEM_MIN), VMEM_CAP))
    return pltpu.CompilerParams(dimension_semantics=semantics, vmem_limit_bytes=limit)


def _pick(n, cands):
    for c in cands:
        if n % c == 0:
            return c
    raise ValueError(f"no tile for {n} in {cands}")


def _split_bf16(x, parts):
    out = []
    for _ in range(parts):
        h = x.astype(BF16)
        out.append(h)
        x = x - h.astype(F32)
    return out


def _dot_f32(a, b):
    a1, a2 = _split_bf16(a, 2)
    b1, b2 = _split_bf16(b, 2)
    d = functools.partial(jnp.dot, preferred_element_type=F32)
    return d(a1, b1) + (d(a1, b2) + d(a2, b1))


def _rms(x, w):
    return x * lax.rsqrt(jnp.mean(x * x, axis=-1, keepdims=True) + NORM_EPS) * w


def _silu(x):
    return x * jax.nn.sigmoid(x)


def _norm_kernel(*refs, n_delta, want_x, n_exp):
    it = iter(refs)
    x_ref = next(it)
    d_refs = [next(it) for _ in range(n_delta)]
    w_ref = next(it)
    wr_ref = next(it) if n_exp else None
    xo_ref = next(it) if want_x else None
    no_ref = next(it)
    x = x_ref[...]
    if n_delta == 1:
        x = x + d_refs[0][...]
    elif n_delta == 2:
        x = x + (d_refs[0][...] + d_refs[1][...])
    if want_x:
        xo_ref[...] = x
    y = _rms(x, w_ref[...])
    no_ref[...] = y.astype(no_ref.dtype)
    if n_exp:
        ti_ref, tp_ref = next(it), next(it)
        logits = _dot_f32(y, wr_ref[...])
        col = lax.broadcasted_iota(jnp.int32, logits.shape, 1)
        colf = col.astype(F32)
        logits = jnp.where(col < n_exp, logits, -jnp.inf)
        m1 = jnp.max(logits, axis=1, keepdims=True)
        i1 = jnp.min(jnp.where(logits == m1, colf, float(LANES)), axis=1, keepdims=True)
        l2 = jnp.where(colf == i1, -jnp.inf, logits)
        m2 = jnp.max(l2, axis=1, keepdims=True)
        i2 = jnp.min(jnp.where(l2 == m2, colf, float(LANES)), axis=1, keepdims=True)
        e = jnp.exp(m2 - m1)
        den = 1.0 + e
        ti_ref[...] = jnp.where(col == 0, i1, jnp.where(col == 1, i2, 0.0))
        tp_ref[...] = jnp.where(col == 0, 1.0 / den, jnp.where(col == 1, e / den, 0.0))


def _norm(x, deltas, w, *, want_x, out_dtype, w_router=None):
    R, D = x.shape
    tr = _pick(R, (208, 160, 144, 136, 128, 64, 48, 32, 16))
    n_exp = 0 if w_router is None else w_router.shape[1]
    row = pl.BlockSpec((tr, D), lambda i: (i, 0))
    args = [x, *deltas, w.reshape(1, D).astype(F32)]
    in_specs = [row] * (1 + len(deltas)) + [pl.BlockSpec((1, D), lambda i: (0, 0))]
    if n_exp:
        args.append(jnp.pad(w_router.astype(F32), ((0, 0), (0, LANES - n_exp))))
        in_specs.append(pl.BlockSpec((D, LANES), lambda i: (0, 0)))
    out_shape, out_specs = [], []
    if want_x:
        out_shape.append(jax.ShapeDtypeStruct((R, D), F32))
        out_specs.append(row)
    out_shape.append(jax.ShapeDtypeStruct((R, D), out_dtype))
    out_specs.append(row)
    if n_exp:
        lane = pl.BlockSpec((tr, LANES), lambda i: (i, 0))
        out_shape += [jax.ShapeDtypeStruct((R, LANES), F32)] * 2
        out_specs += [lane, lane]
    est = 2 * tr * D * 4 * (3 + len(deltas)) + (D * LANES * 8 if n_exp else 0) + 4 * tr * D * 4
    return pl.pallas_call(
        functools.partial(_norm_kernel, n_delta=len(deltas), want_x=want_x, n_exp=n_exp),
        grid=(R // tr,), in_specs=in_specs, out_specs=out_specs, out_shape=out_shape,
        compiler_params=_cparams(("arbitrary",), est), name="norm")(*args)


def _gmm_kernel(te_ref, nv_ref, tv_ref, *refs, n_w, epilogue, has_partial, has_scale, half_rows):
    it = iter(refs)
    a_ref = next(it)
    w_refs = [next(it) for _ in range(n_w)]
    p_ref = next(it) if has_partial else None
    s_ref = next(it) if has_scale else None
    cos_ref, sin_ref = (next(it), next(it)) if epilogue == "rope" else (None, None)
    o_ref = next(it)
    wbf = [next(it) for _ in range(n_w)]
    tm = a_ref.shape[0]
    i = pl.program_id(1)
    is_new = jnp.logical_or(i == 0, te_ref[i] != te_ref[jnp.maximum(i - 1, 0)])

    @pl.when(is_new)
    def _():
        for w_ref, s in zip(w_refs, wbf):
            s[...] = w_ref[0].astype(BF16)

    def compute(rows):
        rs = slice(0, rows)
        a = a_ref[rs, :]
        acc = jnp.dot(a, wbf[0][...], preferred_element_type=F32)
        if epilogue == "swiglu":
            acc = _silu(acc) * jnp.dot(a, wbf[1][...], preferred_element_type=F32)
        if has_partial:
            acc = p_ref[rs, :] + acc
        if has_scale:
            acc = acc * s_ref[rs, :]
        if epilogue == "rope":
            cos, sin = cos_ref[rs, :], sin_ref[rs, :]
            for g in range(acc.shape[1] // LANES):
                t = acc[:, g * LANES:(g + 1) * LANES]
                o_ref[rs, g * LANES:(g + 1) * LANES] = (
                    t * cos + pltpu.roll(t, LANES // 2, 1) * sin).astype(o_ref.dtype)
        else:
            o_ref[rs, :] = acc.astype(o_ref.dtype)
        if rows < tm:
            o_ref[rows:, :] = jnp.zeros((tm - rows, o_ref.shape[1]), o_ref.dtype)

    live = i < nv_ref[0]
    if half_rows:
        full = tv_ref[i] > tm // 2
        pl.when(jnp.logical_and(live, full))(functools.partial(compute, tm))
        pl.when(jnp.logical_and(live, jnp.logical_not(full)))(functools.partial(compute, tm // 2))
    else:
        pl.when(live)(functools.partial(compute, tm))

    @pl.when(jnp.logical_not(live))
    def _():
        o_ref[...] = jnp.zeros_like(o_ref)


def _gmm(a, ws, *, tm, tn, n_out=None, w_col0=0, kb=0, k_blk=None, te=None, nv=None, tv=None,
         epilogue="plain", out_dtype=F32, partial=None, scale=None, rope=None, name="gmm"):
    R = a.shape[0]
    ws = [w if w.ndim == 3 else w[None] for w in ws]
    k_blk = a.shape[1] if k_blk is None else k_blk
    n_out = ws[0].shape[2] if n_out is None else n_out
    assert R % tm == 0 and n_out % tn == 0 and w_col0 % tn == 0
    T = R // tm
    half_rows = tv is not None
    if te is None:
        te = jnp.zeros((T,), jnp.int32)
        nv = jnp.full((1,), T, jnp.int32)
    if tv is None:
        tv = jnp.full((T,), tm, jnp.int32)
    c0 = w_col0 // tn

    def a_map(j, i, te_r, nv_r, tv_r):
        return (jnp.minimum(i, nv_r[0] - 1), kb)

    def w_map(j, i, te_r, nv_r, tv_r):
        return (te_r[i], kb, j + c0)

    def o_map(j, i, te_r, nv_r, tv_r):
        return (i, j)

    def r_map(j, i, te_r, nv_r, tv_r):
        return (i, 0)

    args = [a, *ws]
    in_specs = [pl.BlockSpec((tm, k_blk), a_map)] + [pl.BlockSpec((1, k_blk, tn), w_map)] * len(ws)
    est = 2 * tm * k_blk * a.dtype.itemsize + len(ws) * k_blk * max(tn, LANES) * (2 * 4 + 2)
    est += tm * tn * (2 * jnp.dtype(out_dtype).itemsize + 4 * (2 + len(ws)))
    if partial is not None:
        args.append(partial)
        in_specs.append(pl.BlockSpec((tm, tn), o_map))
        est += 2 * tm * tn * 4
    if scale is not None:
        args.append(scale.reshape(R, 1))
        in_specs.append(pl.BlockSpec((tm, 1), r_map))
        est += 2 * tm * LANES * 4
    if rope is not None:
        args += [rope[0], rope[1]]
        in_specs += [pl.BlockSpec((tm, LANES), r_map)] * 2
        est += 4 * tm * LANES * 4
    grid_spec = pltpu.PrefetchScalarGridSpec(
        num_scalar_prefetch=3, grid=(n_out // tn, T), in_specs=in_specs,
        out_specs=pl.BlockSpec((tm, tn), o_map),
        scratch_shapes=[pltpu.VMEM((k_blk, tn), BF16) for _ in ws])
    return pl.pallas_call(
        functools.partial(_gmm_kernel, n_w=len(ws), epilogue=epilogue,
                          has_partial=partial is not None, has_scale=scale is not None, half_rows=half_rows),
        grid_spec=grid_spec, out_shape=jax.ShapeDtypeStruct((R, n_out), out_dtype),
        compiler_params=_cparams(("arbitrary", "arbitrary"), est), name=name)(te, nv, tv, *args)


def _rec_block(q, k, v, g, st_ref, hh, sub):
    T, dk = q.shape
    rr = lax.broadcasted_iota(jnp.int32, (T, T), 0)
    cc = lax.broadcasted_iota(jnp.int32, (T, T), 1)
    tril = jnp.where(cc <= rr, 1.0, 0.0).astype(BF16)
    cum = sum(jnp.dot(tril, part, preferred_element_type=F32) for part in _split_bf16(g, 3))
    last = cum[T - 1:T, :]
    st = st_ref[0, hh]
    o_inter = lax.dot_general((q * jnp.exp(cum)).astype(BF16), st.astype(BF16), NT,
                              preferred_element_type=F32)
    k_end = (k * jnp.exp(last - cum)).astype(BF16)
    vb = v.astype(BF16)
    st_ref[0, hh] = st * jnp.exp(last) + lax.dot_general(vb, k_end, TN, preferred_element_type=F32)
    row_in_sub = lax.broadcasted_iota(jnp.int32, (sub, dk), 0)
    lane_in_sub = lax.broadcasted_iota(jnp.int32, (sub, sub), 1)
    outs = []
    for a in range(T // sub):
        lo, hi = a * sub, (a + 1) * sub
        qa, ka, ca = q[lo:hi], k[lo:hi], cum[lo:hi]
        att = jnp.zeros((sub, sub), F32)
        for j in range(sub):
            decay = jnp.where(row_in_sub >= j, jnp.exp(ca - ca[j:j + 1, :]), 0.0)
            col = jnp.sum(qa * ka[j:j + 1, :] * decay, axis=1, keepdims=True)
            att = jnp.where(lane_in_sub == j, col, att)
        oa = o_inter[lo:hi] + jnp.dot(att.astype(BF16), vb[lo:hi], preferred_element_type=F32)
        if a > 0:
            cb = cum[lo - 1:lo, :]
            qt = (qa * jnp.exp(ca - cb)).astype(BF16)
            kt = (k[:lo] * jnp.exp(cb - cum[:lo])).astype(BF16)
            att_prev = lax.dot_general(qt, kt, NT, preferred_element_type=F32)
            oa = oa + jnp.dot(att_prev.astype(BF16), vb[:lo], preferred_element_type=F32)
        outs.append(oa)
    return outs[0] if len(outs) == 1 else jnp.concatenate(outs, axis=0)


def _rec_kernel(*refs, mode, n_rows, blk, n_valid, has_init, slot, hp, dk, dv):
    it = iter(refs)
    if mode == "gla":
        q_ref, k_ref, v_ref, gl_ref, wg_ref, bg_ref = (next(it) for _ in range(6))
    else:
        q_ref, f_ref, v_ref, lb_ref = (next(it) for _ in range(4))
    og_ref, nw_ref = next(it), next(it)
    s0_ref = next(it) if has_init else None
    o_ref, st_ref = next(it), next(it)
    if has_init:
        st_ref[...] = s0_ref[...]
    else:
        st_ref[...] = jnp.zeros_like(st_ref)
    pad = blk - n_rows if n_rows < blk else 0

    def load(ref, r0, hh, width):
        cols = slice(hh * width, (hh + 1) * width)
        if pad:
            x = ref[:, cols]
            return jnp.concatenate([x, jnp.zeros((pad, width), x.dtype)], axis=0)
        return ref[pl.ds(r0, blk), cols]

    def body(ib, carry):
        r0 = pl.multiple_of(ib * blk, SUB)
        for hh in range(hp):
            kc = slice(hh * dk, (hh + 1) * dk)
            v = load(v_ref, r0, hh, dv)
            if mode == "gla":
                q = load(q_ref, r0, hh, dk) * dk ** -0.5
                k = load(k_ref, r0, hh, dk)
                z = _dot_f32(load(gl_ref, r0, 0, gl_ref.shape[1]), wg_ref[:, kc]) + bg_ref[:, kc]
                g = (jnp.minimum(z, 0.0) - jnp.log1p(jnp.exp(-jnp.abs(z)))) / GLA_TAU
            else:
                lg = lb_ref[:, kc]
                ex = jnp.exp(lg - jnp.max(lg, axis=0, keepdims=True))
                sm = ex / jnp.sum(ex, axis=0, keepdims=True)
                lb = jnp.sum(sm[:slot + 1], axis=0, keepdims=True)
                f = lb + (1.0 - lb) * jax.nn.sigmoid(load(f_ref, r0, hh, dk))
                q = _silu(load(q_ref, r0, hh, dk))
                k = 1.0 - f
                g = jnp.log(f)
            if n_valid < blk:
                live = lax.broadcasted_iota(jnp.int32, (blk, dk), 0) < n_valid
                k = jnp.where(live, k, 0.0)
                g = jnp.where(live, g, 0.0)
            o = _rec_block(q, k, v, g, st_ref, hh, SUB)
            y = _rms(o, nw_ref[...]) * _silu(load(og_ref, r0, hh, dv))
            vc = slice(hh * dv, (hh + 1) * dv)
            if pad:
                o_ref[:, vc] = y[:n_rows].astype(o_ref.dtype)
            else:
                o_ref[pl.ds(r0, blk), vc] = y.astype(o_ref.dtype)
        return carry

    n_blk = 1 if pad else n_rows // blk
    if n_blk == 1:
        body(0, 0)
    else:
        lax.fori_loop(0, n_blk, body, 0)


def _recurrence(mode, srcs, norm_w, s0t, *, n_batch, n_heads, hp, n_rows, row0, n_valid, dk, dv,
                out_dtype, slot=0):
    rb0 = row0 // n_rows
    assert row0 % n_rows == 0 and n_heads % hp == 0
    blk = SUB if n_rows < SUB else _pick(n_rows, (64, 48, 32, 16))
    args, in_specs = [], []
    est = 0
    for src in srcs:
        args.append(src[1])
        if src[0] == "head":
            _, arr, width, cb0 = src
            assert cb0 % hp == 0
            in_specs.append(pl.BlockSpec((n_rows, width * hp), lambda b, h, c=cb0 // hp: (rb0 + b, c + h)))
            est += 2 * n_rows * width * hp * 4
        elif src[0] == "rows":
            in_specs.append(pl.BlockSpec((n_rows, src[1].shape[1]), lambda b, h: (rb0 + b, 0)))
            est += 2 * n_rows * LANES * 4
        else:
            _, arr, width = src
            in_specs.append(pl.BlockSpec((arr.shape[0], width * hp), lambda b, h: (0, h)))
    args.append(norm_w.reshape(1, dv).astype(F32))
    in_specs.append(pl.BlockSpec((1, dv), lambda b, h: (0, 0)))
    st_spec = pl.BlockSpec((1, hp, dv, dk), lambda b, h: (b, h, 0, 0))
    if s0t is not None:
        args.append(s0t)
        in_specs.append(st_spec)
    est += 4 * hp * dv * dk * 4 + 2 * n_rows * hp * dv * 4 + 64 * blk * max(dk, dv) * 4
    return pl.pallas_call(
        functools.partial(_rec_kernel, mode=mode, n_rows=n_rows, blk=blk, n_valid=n_valid,
                          has_init=s0t is not None, slot=slot, hp=hp, dk=dk, dv=dv),
        grid=(n_batch, n_heads // hp), in_specs=in_specs,
        out_specs=[pl.BlockSpec((n_rows, hp * dv), lambda b, h: (b, h)), st_spec],
        out_shape=[jax.ShapeDtypeStruct((n_batch * n_rows, n_heads * dv), out_dtype),
                   jax.ShapeDtypeStruct((n_batch, n_heads, dv, dk), F32)],
        compiler_params=_cparams(("arbitrary", "arbitrary"), est), name="rec_" + mode)(*args)


def _lambda(lam_ref, lam_init):
    lv = lam_ref[...]
    return (jnp.exp(jnp.sum(lv[0:1] * lv[1:2], axis=1, keepdims=True))
            - jnp.exp(jnp.sum(lv[2:3] * lv[3:4], axis=1, keepdims=True)) + lam_init)


def _attn_prompt_kernel(q_ref, k_ref, v_ref, lam_ref, hn_ref, o_ref, *, lam_init, n_q):
    tq, dh2 = q_ref.shape
    dh = dh2 // 2
    lp = k_ref.shape[1]
    lam = _lambda(lam_ref, lam_init)
    q = q_ref[...]
    qi = pl.program_id(2)

    def attend(n_keys):
        rows = qi * tq + lax.broadcasted_iota(jnp.int32, (tq, n_keys), 0)
        causal = lax.broadcasted_iota(jnp.int32, (tq, n_keys), 1) <= rows
        probs = []
        for r in range(2):
            kr = k_ref[0, 0:n_keys, r * dh:(r + 1) * dh].astype(BF16)
            s = lax.dot_general(q[:, r * dh:(r + 1) * dh].astype(BF16), kr, NT,
                                preferred_element_type=F32) * dh ** -0.5
            s = jnp.where(causal, s, -jnp.inf)
            e = jnp.exp(s - jnp.max(s, axis=1, keepdims=True))
            probs.append(e * (1.0 / jnp.sum(e, axis=1, keepdims=True)))
        a = probs[0] - lam * probs[1]
        o = jnp.dot(a.astype(BF16), v_ref[0, 0:n_keys, :].astype(BF16), preferred_element_type=F32)
        o_ref[...] = (_rms(o, hn_ref[...]) * (1.0 - lam_init)).astype(o_ref.dtype)

    for i in range(n_q):
        n_keys = min(lp, -(-(i + 1) * tq // LANES) * LANES)
        pl.when(qi == i)(functools.partial(attend, n_keys))


def _attn_prompt(q, kpad, vpad, lam_vec, head_norm, *, n_batch, seq, n_heads, dh, lam_init):
    lp = kpad.shape[1]
    tq = _pick(seq, (344, 256, 128, 64, 16))
    nq = seq // tq
    est = 4 * lp * 2 * dh * 4 + 4 * tq * 2 * dh * 4 + 8 * tq * lp * 4
    return pl.pallas_call(
        functools.partial(_attn_prompt_kernel, lam_init=lam_init, n_q=nq),
        grid=(n_batch, n_heads, nq),
        in_specs=[pl.BlockSpec((tq, 2 * dh), lambda b, h, i: (b * nq + i, h)),
                  pl.BlockSpec((1, lp, 2 * dh), lambda b, h, i: (b, 0, h)),
                  pl.BlockSpec((1, lp, 2 * dh), lambda b, h, i: (b, 0, h)),
                  pl.BlockSpec((4, dh), lambda b, h, i: (0, 0)),
                  pl.BlockSpec((1, 2 * dh), lambda b, h, i: (0, 0))],
        out_specs=pl.BlockSpec((tq, 2 * dh), lambda b, h, i: (b * nq + i, h)),
        out_shape=jax.ShapeDtypeStruct((n_batch * seq, n_heads * 2 * dh), BF16),
        compiler_params=_cparams(("arbitrary",) * 3, est), name="attn_prompt")(
            q, kpad, vpad, lam_vec, head_norm.reshape(1, 2 * dh).astype(F32))


def _attn_paged_kernel(pt_ref, q_ref, *refs, n_heads, dh, dec_seq, lam_init, pps):
    n_kg, n_vg = 2 * n_heads // HEAD_GROUP, n_heads // HEAD_GROUP
    n_k, n_v = pps * n_kg, pps * 2 * n_vg
    k_refs, v_refs = refs[:n_k], refs[n_k:n_k + n_v]
    kn_ref, vn_ref, lam_ref, hn_ref, o_ref, qbd, m_s, d_s, acc, kflat, vflat = refs[n_k + n_v:]
    p = pl.program_id(1)
    n_rows = qbd.shape[0]
    grp = 2 * SAMPLE_ROWS

    @pl.when(p == 0)
    def _():
        q8 = q_ref[...] * dh ** -0.5
        qt = jnp.concatenate([q8] * (2 * n_heads), axis=0)
        rg = lax.broadcasted_iota(jnp.int32, qt.shape, 0) >> (SAMPLE_ROWS.bit_length() - 1)
        cg = lax.broadcasted_iota(jnp.int32, qt.shape, 1) >> (dh.bit_length() - 1)
        qbd[...] = jnp.where(rg == cg, qt, 0.0).astype(BF16)
        m_s[...] = jnp.full_like(m_s, -jnp.inf)
        d_s[...] = jnp.zeros_like(d_s)
        acc[...] = jnp.zeros_like(acc)

    def merge(kt, vt, valid):
        s = lax.dot_general(qbd[...], kt.astype(BF16), NT, preferred_element_type=F32)
        if valid is not None:
            s = jnp.where(valid, s, -jnp.inf)
        m_old = m_s[...]
        m_new = jnp.maximum(m_old, jnp.max(s, axis=1, keepdims=True))
        corr = jnp.exp(m_old - m_new)
        pe = jnp.exp(s - m_new)
        d_s[...] = d_s[...] * corr + jnp.sum(pe, axis=1, keepdims=True)
        m_s[...] = m_new
        res = jnp.dot(pe.astype(BF16), vt.astype(BF16), preferred_element_type=F32)
        for h in range(n_heads):
            r = slice(h * grp, (h + 1) * grp)
            acc[r, :] = acc[r, :] * corr[r] + res[r, h * 2 * dh:(h + 1) * 2 * dh]

    page = kflat.shape[0] // pps
    for pg in range(pps):
        keys = slice(pg * page, (pg + 1) * page)
        for c in range(n_kg):
            rows = k_refs[pg * n_kg + c].reshape(page * HEAD_GROUP, dh)
            for i in range(HEAD_GROUP):
                g = c * HEAD_GROUP + i
                kflat[keys, g * dh:(g + 1) * dh] = rows[pl.ds(i, page, stride=HEAD_GROUP), :].astype(BF16)
        for c in range(2 * n_vg):
            rows = v_refs[pg * 2 * n_vg + c].reshape(page * HEAD_GROUP, dh)
            for i in range(HEAD_GROUP):
                col = 2 * ((c // 2) * HEAD_GROUP + i) + c % 2
                vflat[keys, col * dh:(col + 1) * dh] = rows[pl.ds(i, page, stride=HEAD_GROUP), :].astype(BF16)
    merge(kflat[...], vflat[...], None)

    @pl.when(p == pl.num_programs(1) - 1)
    def _():
        zeros = jnp.zeros_like(kn_ref[...])
        kn = jnp.concatenate([kn_ref[...], zeros], axis=0)
        vn = jnp.concatenate([vn_ref[...], zeros], axis=0)
        tok = lax.broadcasted_iota(jnp.int32, (n_rows, 2 * SAMPLE_ROWS), 0) & (SAMPLE_ROWS - 1)
        key = lax.broadcasted_iota(jnp.int32, (n_rows, 2 * SAMPLE_ROWS), 1)
        merge(kn, vn, jnp.logical_and(key <= tok, key < dec_seq))
        lam = _lambda(lam_ref, lam_init)
        o = acc[...] / d_s[...]
        for h in range(n_heads):
            oh = o[h * grp:h * grp + SAMPLE_ROWS] - lam * o[h * grp + SAMPLE_ROWS:(h + 1) * grp]
            o_ref[:, h * 2 * dh:(h + 1) * 2 * dh] = _rms(oh, hn_ref[...]) * (1.0 - lam_init)


def _attn_paged(q, k, v, cache_k, cache_v, page_table, layer, lam_vec, head_norm, *, row0, n_heads, dh,
                dec_seq, lam_init):
    n_dec, n_pages = page_table.shape
    n_pool, page = cache_k.shape[1], cache_k.shape[2]
    dm = n_heads * 2 * dh
    assert n_heads % HEAD_GROUP == 0
    ck = cache_k.reshape(-1, page, 2 * n_heads, dh)
    cv = cache_v.reshape(-1, page, n_heads, 2 * dh)
    rb0 = row0 // SAMPLE_ROWS
    n_rows = 2 * n_heads * SAMPLE_ROWS
    base = layer * n_pool
    n_kg, n_vg = 2 * n_heads // HEAD_GROUP, n_heads // HEAD_GROUP
    pps = PAGES_PER_STEP if n_pages % PAGES_PER_STEP == 0 else 1

    def page_spec(pg, group, lane_half):
        return pl.BlockSpec((1, page, HEAD_GROUP, dh),
                            lambda b, p, pt: (base + pt[b * n_pages + p * pps + pg], 0, group, lane_half))

    def new_map(b, p, pt):
        return (rb0 + b, 0)

    grid_spec = pltpu.PrefetchScalarGridSpec(
        num_scalar_prefetch=1, grid=(n_dec, n_pages // pps),
        in_specs=[pl.BlockSpec((SAMPLE_ROWS, dm), new_map),
                  *[page_spec(pg, c, 0) for pg in range(pps) for c in range(n_kg)],
                  *[page_spec(pg, c // 2, c % 2) for pg in range(pps) for c in range(2 * n_vg)],
                  pl.BlockSpec((SAMPLE_ROWS, dm), new_map), pl.BlockSpec((SAMPLE_ROWS, dm), new_map),
                  pl.BlockSpec((4, dh), lambda b, p, pt: (0, 0)),
                  pl.BlockSpec((1, 2 * dh), lambda b, p, pt: (0, 0))],
        out_specs=pl.BlockSpec((SAMPLE_ROWS, dm), lambda b, p, pt: (b, 0)),
        scratch_shapes=[pltpu.VMEM((n_rows, dm), BF16), pltpu.VMEM((n_rows, 1), F32),
                        pltpu.VMEM((n_rows, 1), F32), pltpu.VMEM((n_rows, 2 * dh), F32),
                        pltpu.VMEM((pps * page, dm), BF16), pltpu.VMEM((pps * page, dm), BF16)])
    est = pps * (4 * page * dm * 4 + 3 * page * dm * 2) + 3 * n_rows * dm * 4
    return pl.pallas_call(
        functools.partial(_attn_paged_kernel, n_heads=n_heads, dh=dh, dec_seq=dec_seq, lam_init=lam_init,
                          pps=pps),
        grid_spec=grid_spec, out_shape=jax.ShapeDtypeStruct((n_dec * SAMPLE_ROWS, dm), F32),
        compiler_params=_cparams(("arbitrary", "arbitrary"), est), name="attn_paged")(
            page_table.reshape(-1).astype(jnp.int32), q, *[ck] * (pps * n_kg), *[cv] * (pps * 2 * n_vg), k, v,
            lam_vec,
            head_norm.reshape(1, 2 * dh).astype(F32))


def _moe_plan(top_idx, top_prob, n_exp, tm):
    n, k = top_idx.shape
    e_flat = top_idx.reshape(-1)
    order = jnp.argsort(e_flat, stable=True)
    sorted_e = e_flat[order]
    counts = jnp.sum((e_flat[:, None] == jnp.arange(n_exp, dtype=jnp.int32)[None, :]).astype(jnp.int32), axis=0)
    padded = (counts + tm - 1) // tm * tm
    p_end = jnp.cumsum(padded)
    p_start = p_end - padded
    start = jnp.cumsum(counts) - counts
    dest = (p_start[sorted_e] + jnp.arange(n * k, dtype=jnp.int32) - start[sorted_e]).astype(jnp.int32)
    n_tiles_max = (n * k + n_exp * (tm - 1)) // tm
    rows = n_tiles_max * tm
    src_tok = jnp.zeros((rows,), jnp.int32).at[dest].set((order // k).astype(jnp.int32))
    scale = jnp.zeros((rows,), F32).at[dest].set(top_prob.reshape(-1)[order])
    pos = jnp.zeros((n * k,), jnp.int32).at[order].set(dest).reshape(n, k)
    n_tiles = (p_end[-1] // tm).astype(jnp.int32)
    tile_e = jnp.searchsorted(p_end, jnp.arange(n_tiles_max, dtype=jnp.int32) * tm, side="right")
    tile_e = jnp.minimum(tile_e, n_exp - 1).astype(jnp.int32)
    last_e = tile_e[jnp.maximum(n_tiles - 1, 0)]
    tile_e = jnp.where(jnp.arange(n_tiles_max) < n_tiles, tile_e, last_e)
    tile_rows = jnp.clip(p_start[tile_e] + counts[tile_e] - jnp.arange(n_tiles_max, dtype=jnp.int32) * tm, 0, tm)
    return src_tok, scale, pos, tile_e, n_tiles.reshape(1), tile_rows.astype(jnp.int32)


def _tiles(R):
    tm = _pick(R, (832, 640, 416, 272, 256, 136, 128, 64, 16))
    return tm


def kernel(x_prompt, x_sample, state_gla, state_hgrn, cache_k, cache_v, page_table, meta_tokens, norm_mix, norm_ffn, final_norm, a_w_in, a_gla_w_gate_up, a_gla_b_gate, a_gla_norm, a_hgrn_lb_logits, a_hgrn_norm, a_w_out, c_w_q, c_w_k, c_w_v, c_lambda_q1, c_lambda_k1, c_lambda_q2, c_lambda_k2, c_head_norm, c_w_o, ffn_w_gate, ffn_w_up, ffn_w_down, moe_w_router, moe_w_gate, moe_w_up, moe_w_down):
    B, seq_in, D = x_prompt.shape
    DB, dec_seq, _ = x_sample.shape
    n_meta = meta_tokens.shape[0]
    L = n_meta + seq_in
    BL = B * L
    R = BL + DB * SAMPLE_ROWS
    depth = norm_mix.shape[0]
    GH, gdk, gdv = state_gla.shape[2:]
    HH, hdk, hdv = state_hgrn.shape[2:]
    rank = a_gla_w_gate_up.shape[1]
    n_heads = cache_v.shape[3]
    dh = cache_k.shape[4]
    n_exp = moe_w_router.shape[2]
    past_len = page_table.shape[1] * cache_k.shape[2]
    assert dec_seq <= SAMPLE_ROWS and dh == LANES and BL % SAMPLE_ROWS == 0
    tm = _tiles(R)

    meta = jnp.broadcast_to(meta_tokens[None].astype(F32), (B, n_meta, D))
    xp = jnp.concatenate([meta, x_prompt], axis=1).reshape(BL, D)
    xs = jnp.pad(x_sample, ((0, 0), (0, SAMPLE_ROWS - dec_seq), (0, 0))).reshape(DB * SAMPLE_ROWS, D)
    x = jnp.concatenate([xp, xs], axis=0)

    pos_s = jnp.where(jnp.arange(SAMPLE_ROWS) < dec_seq, past_len + jnp.arange(SAMPLE_ROWS), 0)
    pos = jnp.concatenate([jnp.tile(jnp.arange(L, dtype=jnp.int32), B),
                           jnp.tile(pos_s.astype(jnp.int32), DB)])
    inv = ROPE_THETA ** (-jnp.arange(dh // 2, dtype=F32) * 2.0 / dh)
    ang = pos.astype(F32)[:, None] * inv[None, :]
    rope = (jnp.concatenate([jnp.cos(ang), jnp.cos(ang)], axis=1),
            jnp.concatenate([-jnp.sin(ang), jnp.sin(ang)], axis=1))

    c_low = 2 * GH * gdk + GH * gdv
    deltas = []
    gla_p, gla_s, hgrn_p, hgrn_s, k_p, k_s, v_p, v_s = ([] for _ in range(8))
    for i in range(depth):
        j = i // 2
        if deltas:
            x, xn = _norm(x, deltas, norm_mix[i], want_x=True, out_dtype=BF16)
        else:
            (xn,) = _norm(x, [], norm_mix[i], want_x=False, out_dtype=BF16)
        if i % 2 == 0:
            w_in = a_w_in[j]
            pa = _gmm(xn, [w_in], tm=tm, tn=512 if c_low % 512 == 0 else LANES, n_out=c_low, name="in_a")
            glow = _gmm(xn, [w_in[:, c_low:c_low + rank]], tm=tm, tn=rank, name="in_low")
            w_tail = w_in[:, c_low + rank:]
            pb = _gmm(xn, [w_tail], tm=tm, tn=512 if w_tail.shape[1] % 512 == 0 else LANES, name="in_b")
            wgu = a_gla_w_gate_up[j].astype(F32)
            bg = a_gla_b_gate[j].reshape(1, -1).astype(F32)
            c_hq = GH * gdv
            c_hf, c_hi, c_ho = c_hq + HH * hdk, c_hq + 2 * HH * hdk, c_hq + 2 * HH * hdk + HH * hdv
            hp_h = 4 if HH % 4 == 0 else 1
            outs = []
            for (n_b, n_rows, row0, n_valid, s_gla, s_hgrn, odt) in (
                    (B, L, 0, L, None, None, BF16),
                    (DB, SAMPLE_ROWS, BL, dec_seq, jnp.swapaxes(state_gla[j], -1, -2),
                     jnp.swapaxes(state_hgrn[j], -1, -2), F32)):
                gsrc = [("head", pa, gdk, 0), ("head", pa, gdk, GH), ("head", pa, gdv, 2 * GH * gdk // gdv),
                        ("rows", glow), ("cols", wgu, gdk), ("cols", bg, gdk), ("head", pb, gdv, 0)]
                oa, sa = _recurrence("gla", gsrc, a_gla_norm[j], s_gla, n_batch=n_b, n_heads=GH, hp=1,
                                     n_rows=n_rows, row0=row0, n_valid=n_valid, dk=gdk, dv=gdv, out_dtype=odt)
                hsrc = [("head", pb, hdk, c_hq // hdk), ("head", pb, hdk, c_hf // hdk),
                        ("head", pb, hdv, c_hi // hdv), ("cols", a_hgrn_lb_logits.astype(F32), hdk),
                        ("head", pb, hdv, c_ho // hdv)]
                ob, sb = _recurrence("hgrn", hsrc, a_hgrn_norm[j], s_hgrn, n_batch=n_b, n_heads=HH, hp=hp_h,
                                     n_rows=n_rows, row0=row0, n_valid=n_valid, dk=hdk, dv=hdv, out_dtype=odt,
                                     slot=j)
                outs.append((oa, ob, jnp.swapaxes(sa, -1, -2), jnp.swapaxes(sb, -1, -2)))
            gla_p.append(outs[0][2]); hgrn_p.append(outs[0][3])
            gla_s.append(outs[1][2]); hgrn_s.append(outs[1][3])
            o = jnp.concatenate([jnp.concatenate([outs[0][0], outs[0][1]], axis=1),
                                 jnp.concatenate([outs[1][0], outs[1][1]], axis=1).astype(BF16)], axis=0)
            mix = _gmm(o, [a_w_out[j]], tm=tm, tn=512 if D % 512 == 0 else LANES, name="w_out")
            x, xn = _norm(x, [mix], norm_ffn[i], want_x=True, out_dtype=BF16)
            dff = ffn_w_gate.shape[2]
            hmid = _gmm(xn, [ffn_w_gate[j], ffn_w_up[j]], tm=tm, tn=256 if dff % 256 == 0 else LANES,
                        epilogue="swiglu", out_dtype=BF16, name="ffn_up")
            down = dict(tm=tm // 2 if tm % 32 == 0 else tm, tn=512 if D % 512 == 0 else LANES, k_blk=dff // 2)
            part = _gmm(hmid, [ffn_w_down[j]], kb=0, name="ffn_down0", **down)
            ffn = _gmm(hmid, [ffn_w_down[j]], kb=1, partial=part, name="ffn_down1", **down)
            deltas = [ffn]
        else:
            lam_init = 0.8 - 0.6 * math.exp(-0.3 * i)
            tn = 512 if D % 512 == 0 else LANES
            q = _gmm(xn, [c_w_q[j]], tm=tm, tn=tn, epilogue="rope", rope=rope, name="w_q")
            k = _gmm(xn, [c_w_k[j]], tm=tm, tn=tn, epilogue="rope", rope=rope, name="w_k")
            v = _gmm(xn, [c_w_v[j]], tm=tm, tn=tn, name="w_v")
            k_p.append(k[:BL].reshape(B, L, 2 * n_heads, dh))
            v_p.append(v[:BL].reshape(B, L, n_heads, 2 * dh))
            k_s.append(k[BL:].reshape(DB, SAMPLE_ROWS, 2 * n_heads, dh)[:, :dec_seq])
            v_s.append(v[BL:].reshape(DB, SAMPLE_ROWS, n_heads, 2 * dh)[:, :dec_seq])
            lam_vec = jnp.stack([c_lambda_q1[j], c_lambda_k1[j], c_lambda_q2[j], c_lambda_k2[j]]).astype(F32)
            lp = -(-L // LANES) * LANES
            kpad = jnp.pad(k[:BL].reshape(B, L, D), ((0, 0), (0, lp - L), (0, 0)))
            vpad = jnp.pad(v[:BL].reshape(B, L, D), ((0, 0), (0, lp - L), (0, 0)))
            o_p = _attn_prompt(q, kpad, vpad, lam_vec, c_head_norm[j], n_batch=B, seq=L, n_heads=n_heads,
                               dh=dh, lam_init=lam_init)
            o_s = _attn_paged(q, k, v, cache_k, cache_v, page_table, j, lam_vec, c_head_norm[j], row0=BL,
                              n_heads=n_heads, dh=dh, dec_seq=dec_seq, lam_init=lam_init)
            o = jnp.concatenate([o_p, o_s.astype(BF16)], axis=0)
            mix = _gmm(o, [c_w_o[j]], tm=tm, tn=tn, name="w_o")
            x, xn, ti, tp = _norm(x, [mix], norm_ffn[i], want_x=True, out_dtype=BF16, w_router=moe_w_router[j])
            tm_e = MOE_ROW_TILE
            src_tok, scale, pos_e, tile_e, n_tiles, tile_rows = _moe_plan(
                ti[:, :2].astype(jnp.int32), tp[:, :2], n_exp, tm_e)
            xs_e = jnp.take(xn, src_tok, axis=0)
            eff = moe_w_gate.shape[3]
            routed = dict(tm=tm_e, te=tile_e, nv=n_tiles, tv=tile_rows)
            h_e = _gmm(xs_e, [moe_w_gate[j], moe_w_up[j]], tn=512 if eff % 512 == 0 else LANES,
                       epilogue="swiglu", out_dtype=BF16, name="moe_up", **routed)
            kh = eff // 2
            tn_d = 512 if D % 512 == 0 else LANES
            part = _gmm(h_e, [moe_w_down[j]], tn=tn_d, kb=0, k_blk=kh, name="moe_down0", **routed)
            y_e = _gmm(h_e, [moe_w_down[j]], tn=tn_d, kb=1, k_blk=kh, partial=part, scale=scale,
                       name="moe_down1", **routed)
            deltas = [jnp.take(y_e, pos_e[:, 0], axis=0), jnp.take(y_e, pos_e[:, 1], axis=0)]
    (y,) = _norm(x, deltas, final_norm, want_x=False, out_dtype=F32)
    y_prompt = y[:BL].reshape(B, L, D)[:, n_meta:]
    y_sample = y[BL:].reshape(DB, SAMPLE_ROWS, D)[:, :dec_seq]
    return (y_prompt, y_sample, jnp.stack(gla_p), jnp.stack(hgrn_p), jnp.stack(k_p), jnp.stack(v_p),
            jnp.stack(gla_s), jnp.stack(hgrn_s), jnp.stack(k_s), jnp.stack(v_s))
```

```python
import functools
import math

import jax
import jax.numpy as jnp
from jax import lax
from jax.experimental import pallas as pl
from jax.experimental.pallas import tpu as pltpu

F32 = jnp.float32
BF16 = jnp.bfloat16
NORM_EPS = 1e-6
ROPE_THETA = 10000.0
GLA_TAU = 16.0
LANES = 128
SAMPLE_ROWS = 8
HEAD_GROUP = 8
PAGES_PER_STEP = 2
SUB = 16
MOE_ROW_TILE = 512
VMEM_CAP = 60 * 2 ** 20
VMEM_MIN = 32 * 2 ** 20
NT = (((1,), (1,)), ((), ()))
TN = (((0,), (0,)), ((), ()))


def _cparams(semantics, est_bytes):
    limit = int(min(max(est_bytes * 5 // 4 + (4 << 20), VMEM_MIN), VMEM_CAP))
    return pltpu.CompilerParams(dimension_semantics=semantics, vmem_limit_bytes=limit)


def _pick(n, cands):
    for c in cands:
        if n % c == 0:
            return c
    raise ValueError(f"no tile for {n} in {cands}")


def _split_bf16(x, parts):
    out = []
    for _ in range(parts):
        h = x.astype(BF16)
        out.append(h)
        x = x - h.astype(F32)
    return out


def _dot_f32(a, b):
    a1, a2 = _split_bf16(a, 2)
    b1, b2 = _split_bf16(b, 2)
    d = functools.partial(jnp.dot, preferred_element_type=F32)
    return d(a1, b1) + (d(a1, b2) + d(a2, b1))


def _rms(x, w):
    return x * lax.rsqrt(jnp.mean(x * x, axis=-1, keepdims=True) + NORM_EPS) * w


def _silu(x):
    return x * jax.nn.sigmoid(x)


def _norm_kernel(*refs, n_delta, want_x, n_exp):
    it = iter(refs)
    x_ref = next(it)
    d_refs = [next(it) for _ in range(n_delta)]
    w_ref = next(it)
    wr_ref = next(it) if n_exp else None
    xo_ref = next(it) if want_x else None
    no_ref = next(it)
    x = x_ref[...]
    if n_delta == 1:
        x = x + d_refs[0][...]
    elif n_delta == 2:
        x = x + (d_refs[0][...] + d_refs[1][...])
    if want_x:
        xo_ref[...] = x
    y = _rms(x, w_ref[...])
    no_ref[...] = y.astype(no_ref.dtype)
    if n_exp:
        ti_ref, tp_ref = next(it), next(it)
        logits = _dot_f32(y, wr_ref[...])
        col = lax.broadcasted_iota(jnp.int32, logits.shape, 1)
        colf = col.astype(F32)
        logits = jnp.where(col < n_exp, logits, -jnp.inf)
        m1 = jnp.max(logits, axis=1, keepdims=True)
        i1 = jnp.min(jnp.where(logits == m1, colf, float(LANES)), axis=1, keepdims=True)
        l2 = jnp.where(colf == i1, -jnp.inf, logits)
        m2 = jnp.max(l2, axis=1, keepdims=True)
        i2 = jnp.min(jnp.where(l2 == m2, colf, float(LANES)), axis=1, keepdims=True)
        e = jnp.exp(m2 - m1)
        den = 1.0 + e
        ti_ref[...] = jnp.where(col == 0, i1, jnp.where(col == 1, i2, 0.0))
        tp_ref[...] = jnp.where(col == 0, 1.0 / den, jnp.where(col == 1, e / den, 0.0))


def _norm(x, deltas, w, *, want_x, out_dtype, w_router=None):
    R, D = x.shape
    tr = _pick(R, (208, 160, 144, 136, 128, 64, 48, 32, 16))
    n_exp = 0 if w_router is None else w_router.shape[1]
    row = pl.BlockSpec((tr, D), lambda i: (i, 0))
    args = [x, *deltas, w.reshape(1, D).astype(F32)]
    in_specs = [row] * (1 + len(deltas)) + [pl.BlockSpec((1, D), lambda i: (0, 0))]
    if n_exp:
        args.append(jnp.pad(w_router.astype(F32), ((0, 0), (0, LANES - n_exp))))
        in_specs.append(pl.BlockSpec((D, LANES), lambda i: (0, 0)))
    out_shape, out_specs = [], []
    if want_x:
        out_shape.append(jax.ShapeDtypeStruct((R, D), F32))
        out_specs.append(row)
    out_shape.append(jax.ShapeDtypeStruct((R, D), out_dtype))
    out_specs.append(row)
    if n_exp:
        lane = pl.BlockSpec((tr, LANES), lambda i: (i, 0))
        out_shape += [jax.ShapeDtypeStruct((R, LANES), F32)] * 2
        out_specs += [lane, lane]
    est = 2 * tr * D * 4 * (3 + len(deltas)) + (D * LANES * 8 if n_exp else 0) + 4 * tr * D * 4
    return pl.pallas_call(
        functools.partial(_norm_kernel, n_delta=len(deltas), want_x=want_x, n_exp=n_exp),
        grid=(R // tr,), in_specs=in_specs, out_specs=out_specs, out_shape=out_shape,
        compiler_params=_cparams(("arbitrary",), est), name="norm")(*args)


def _gmm_kernel(te_ref, nv_ref, tv_ref, *refs, n_w, epilogue, has_partial, has_scale, half_rows):
    it = iter(refs)
    a_ref = next(it)
    w_refs = [next(it) for _ in range(n_w)]
    p_ref = next(it) if has_partial else None
    s_ref = next(it) if has_scale else None
    cos_ref, sin_ref = (next(it), next(it)) if epilogue == "rope" else (None, None)
    o_ref = next(it)
    wbf = [next(it) for _ in range(n_w)]
    tm = a_ref.shape[0]
    i = pl.program_id(1)
    is_new = jnp.logical_or(i == 0, te_ref[i] != te_ref[jnp.maximum(i - 1, 0)])

    @pl.when(is_new)
    def _():
        for w_ref, s in zip(w_refs, wbf):
            s[...] = w_ref[0].astype(BF16)

    def compute(rows):
        rs = slice(0, rows)
        a = a_ref[rs, :]
        acc = jnp.dot(a, wbf[0][...], preferred_element_type=F32)
        if epilogue == "swiglu":
            acc = _silu(acc) * jnp.dot(a, wbf[1][...], preferred_element_type=F32)
        if has_partial:
            acc = p_ref[rs, :] + acc
        if has_scale:
            acc = acc * s_ref[rs, :]
        if epilogue == "rope":
            cos, sin = cos_ref[rs, :], sin_ref[rs, :]
            for g in range(acc.shape[1] // LANES):
                t = acc[:, g * LANES:(g + 1) * LANES]
                o_ref[rs, g * LANES:(g + 1) * LANES] = (
                    t * cos + pltpu.roll(t, LANES // 2, 1) * sin).astype(o_ref.dtype)
        else:
            o_ref[rs, :] = acc.astype(o_ref.dtype)
        if rows < tm:
            o_ref[rows:, :] = jnp.zeros((tm - rows, o_ref.shape[1]), o_ref.dtype)

    live = i < nv_ref[0]
    if half_rows:
        full = tv_ref[i] > tm // 2
        pl.when(jnp.logical_and(live, full))(functools.partial(compute, tm))
        pl.when(jnp.logical_and(live, jnp.logical_not(full)))(functools.partial(compute, tm // 2))
    else:
        pl.when(live)(functools.partial(compute, tm))

    @pl.when(jnp.logical_not(live))
    def _():
        o_ref[...] = jnp.zeros_like(o_ref)


def _gmm(a, ws, *, tm, tn, n_out=None, w_col0=0, kb=0, k_blk=None, te=None, nv=None, tv=None,
         epilogue="plain", out_dtype=F32, partial=None, scale=None, rope=None, name="gmm"):
    R = a.shape[0]
    ws = [w if w.ndim == 3 else w[None] for w in ws]
    k_blk = a.shape[1] if k_blk is None else k_blk
    n_out = ws[0].shape[2] if n_out is None else n_out
    assert R % tm == 0 and n_out % tn == 0 and w_col0 % tn == 0
    T = R // tm
    half_rows = tv is not None
    if te is None:
        te = jnp.zeros((T,), jnp.int32)
        nv = jnp.full((1,), T, jnp.int32)
    if tv is None:
        tv = jnp.full((T,), tm, jnp.int32)
    c0 = w_col0 // tn

    def a_map(j, i, te_r, nv_r, tv_r):
        return (jnp.minimum(i, nv_r[0] - 1), kb)

    def w_map(j, i, te_r, nv_r, tv_r):
        return (te_r[i], kb, j + c0)

    def o_map(j, i, te_r, nv_r, tv_r):
        return (i, j)

    def r_map(j, i, te_r, nv_r, tv_r):
        return (i, 0)

    args = [a, *ws]
    in_specs = [pl.BlockSpec((tm, k_blk), a_map)] + [pl.BlockSpec((1, k_blk, tn), w_map)] * len(ws)
    est = 2 * tm * k_blk * a.dtype.itemsize + len(ws) * k_blk * max(tn, LANES) * (2 * 4 + 2)
    est += tm * tn * (2 * jnp.dtype(out_dtype).itemsize + 4 * (2 + len(ws)))
    if partial is not None:
        args.append(partial)
        in_specs.append(pl.BlockSpec((tm, tn), o_map))
        est += 2 * tm * tn * 4
    if scale is not None:
        args.append(scale.reshape(R, 1))
        in_specs.append(pl.BlockSpec((tm, 1), r_map))
        est += 2 * tm * LANES * 4
    if rope is not None:
        args += [rope[0], rope[1]]
        in_specs += [pl.BlockSpec((tm, LANES), r_map)] * 2
        est += 4 * tm * LANES * 4
    grid_spec = pltpu.PrefetchScalarGridSpec(
        num_scalar_prefetch=3, grid=(n_out // tn, T), in_specs=in_specs,
        out_specs=pl.BlockSpec((tm, tn), o_map),
        scratch_shapes=[pltpu.VMEM((k_blk, tn), BF16) for _ in ws])
    return pl.pallas_call(
        functools.partial(_gmm_kernel, n_w=len(ws), epilogue=epilogue,
                          has_partial=partial is not None, has_scale=scale is not None, half_rows=half_rows),
        grid_spec=grid_spec, out_shape=jax.ShapeDtypeStruct((R, n_out), out_dtype),
        compiler_params=_cparams(("arbitrary", "arbitrary"), est), name=name)(te, nv, tv, *args)


def _rec_block(q, k, v, g, st_ref, hh, sub):
    T, dk = q.shape
    rr = lax.broadcasted_iota(jnp.int32, (T, T), 0)
    cc = lax.broadcasted_iota(jnp.int32, (T, T), 1)
    tril = jnp.where(cc <= rr, 1.0, 0.0).astype(BF16)
    cum = sum(jnp.dot(tril, part, preferred_element_type=F32) for part in _split_bf16(g, 3))
    last = cum[T - 1:T, :]
    st = st_ref[0, hh]
    o_inter = lax.dot_general((q * jnp.exp(cum)).astype(BF16), st.astype(BF16), NT,
                              preferred_element_type=F32)
    k_end = (k * jnp.exp(last - cum)).astype(BF16)
    vb = v.astype(BF16)
    st_ref[0, hh] = st * jnp.exp(last) + lax.dot_general(vb, k_end, TN, preferred_element_type=F32)
    row_in_sub = lax.broadcasted_iota(jnp.int32, (sub, dk), 0)
    lane_in_sub = lax.broadcasted_iota(jnp.int32, (sub, sub), 1)
    outs = []
    for a in range(T // sub):
        lo, hi = a * sub, (a + 1) * sub
        qa, ka, ca = q[lo:hi], k[lo:hi], cum[lo:hi]
        att = jnp.zeros((sub, sub), F32)
        for j in range(sub):
            decay = jnp.where(row_in_sub >= j, jnp.exp(ca - ca[j:j + 1, :]), 0.0)
            col = jnp.sum(qa * ka[j:j + 1, :] * decay, axis=1, keepdims=True)
            att = jnp.where(lane_in_sub == j, col, att)
        oa = o_inter[lo:hi] + jnp.dot(att.astype(BF16), vb[lo:hi], preferred_element_type=F32)
        if a > 0:
            cb = cum[lo - 1:lo, :]
            qt = (qa * jnp.exp(ca - cb)).astype(BF16)
            kt = (k[:lo] * jnp.exp(cb - cum[:lo])).astype(BF16)
            att_prev = lax.dot_general(qt, kt, NT, preferred_element_type=F32)
            oa = oa + jnp.dot(att_prev.astype(BF16), vb[:lo], preferred_element_type=F32)
        outs.append(oa)
    return outs[0] if len(outs) == 1 else jnp.concatenate(outs, axis=0)


def _rec_kernel(*refs, mode, n_rows, blk, n_valid, has_init, slot, hp, dk, dv):
    it = iter(refs)
    if mode == "gla":
        q_ref, k_ref, v_ref, gl_ref, wg_ref, bg_ref = (next(it) for _ in range(6))
    else:
        q_ref, f_ref, v_ref, lb_ref = (next(it) for _ in range(4))
    og_ref, nw_ref = next(it), next(it)
    s0_ref = next(it) if has_init else None
    o_ref, st_ref = next(it), next(it)
    if has_init:
        st_ref[...] = s0_ref[...]
    else:
        st_ref[...] = jnp.zeros_like(st_ref)
    pad = blk - n_rows if n_rows < blk else 0

    def load(ref, r0, hh, width):
        cols = slice(hh * width, (hh + 1) * width)
        if pad:
            x = ref[:, cols]
            return jnp.concatenate([x, jnp.zeros((pad, width), x.dtype)], axis=0)
        return ref[pl.ds(r0, blk), cols]

    def body(ib, carry):
        r0 = pl.multiple_of(ib * blk, SUB)
        for hh in range(hp):
            kc = slice(hh * dk, (hh + 1) * dk)
            v = load(v_ref, r0, hh, dv)
            if mode == "gla":
                q = load(q_ref, r0, hh, dk) * dk ** -0.5
                k = load(k_ref, r0, hh, dk)
                z = _dot_f32(load(gl_ref, r0, 0, gl_ref.shape[1]), wg_ref[:, kc]) + bg_ref[:, kc]
                g = (jnp.minimum(z, 0.0) - jnp.log1p(jnp.exp(-jnp.abs(z)))) / GLA_TAU
            else:
                lg = lb_ref[:, kc]
                ex = jnp.exp(lg - jnp.max(lg, axis=0, keepdims=True))
                sm = ex / jnp.sum(ex, axis=0, keepdims=True)
                lb = jnp.sum(sm[:slot + 1], axis=0, keepdims=True)
                f = lb + (1.0 - lb) * jax.nn.sigmoid(load(f_ref, r0, hh, dk))
                q = _silu(load(q_ref, r0, hh, dk))
                k = 1.0 - f
                g = jnp.log(f)
            if n_valid < blk:
                live = lax.broadcasted_iota(jnp.int32, (blk, dk), 0) < n_valid
                k = jnp.where(live, k, 0.0)
                g = jnp.where(live, g, 0.0)
            o = _rec_block(q, k, v, g, st_ref, hh, SUB)
            y = _rms(o, nw_ref[...]) * _silu(load(og_ref, r0, hh, dv))
            vc = slice(hh * dv, (hh + 1) * dv)
            if pad:
                o_ref[:, vc] = y[:n_rows].astype(o_ref.dtype)
            else:
                o_ref[pl.ds(r0, blk), vc] = y.astype(o_ref.dtype)
        return carry

    n_blk = 1 if pad else n_rows // blk
    if n_blk == 1:
        body(0, 0)
    else:
        lax.fori_loop(0, n_blk, body, 0)


def _recurrence(mode, srcs, norm_w, s0t, *, n_batch, n_heads, hp, n_rows, row0, n_valid, dk, dv,
                out_dtype, slot=0):
    rb0 = row0 // n_rows
    assert row0 % n_rows == 0 and n_heads % hp == 0
    blk = SUB if n_rows < SUB else _pick(n_rows, (64, 48, 32, 16))
    args, in_specs = [], []
    est = 0
    for src in srcs:
        args.append(src[1])
        if src[0] == "head":
            _, arr, width, cb0 = src
            assert cb0 % hp == 0
            in_specs.append(pl.BlockSpec((n_rows, width * hp), lambda b, h, c=cb0 // hp: (rb0 + b, c + h)))
            est += 2 * n_rows * width * hp * 4
        elif src[0] == "rows":
            in_specs.append(pl.BlockSpec((n_rows, src[1].shape[1]), lambda b, h: (rb0 + b, 0)))
            est += 2 * n_rows * LANES * 4
        else:
            _, arr, width = src
            in_specs.append(pl.BlockSpec((arr.shape[0], width * hp), lambda b, h: (0, h)))
    args.append(norm_w.reshape(1, dv).astype(F32))
    in_specs.append(pl.BlockSpec((1, dv), lambda b, h: (0, 0)))
    st_spec = pl.BlockSpec((1, hp, dv, dk), lambda b, h: (b, h, 0, 0))
    if s0t is not None:
        args.append(s0t)
        in_specs.append(st_spec)
    est += 4 * hp * dv * dk * 4 + 2 * n_rows * hp * dv * 4 + 64 * blk * max(dk, dv) * 4
    return pl.pallas_call(
        functools.partial(_rec_kernel, mode=mode, n_rows=n_rows, blk=blk, n_valid=n_valid,
                          has_init=s0t is not None, slot=slot, hp=hp, dk=dk, dv=dv),
        grid=(n_batch, n_heads // hp), in_specs=in_specs,
        out_specs=[pl.BlockSpec((n_rows, hp * dv), lambda b, h: (b, h)), st_spec],
        out_shape=[jax.ShapeDtypeStruct((n_batch * n_rows, n_heads * dv), out_dtype),
                   jax.ShapeDtypeStruct((n_batch, n_heads, dv, dk), F32)],
        compiler_params=_cparams(("arbitrary", "arbitrary"), est), name="rec_" + mode)(*args)


def _lambda(lam_ref, lam_init):
    lv = lam_ref[...]
    return (jnp.exp(jnp.sum(lv[0:1] * lv[1:2], axis=1, keepdims=True))
            - jnp.exp(jnp.sum(lv[2:3] * lv[3:4], axis=1, keepdims=True)) + lam_init)


def _attn_prompt_kernel(q_ref, k_ref, v_ref, lam_ref, hn_ref, o_ref, *, lam_init, n_q):
    tq, dh2 = q_ref.shape
    dh = dh2 // 2
    lp = k_ref.shape[1]
    lam = _lambda(lam_ref, lam_init)
    q = q_ref[...]
    qi = pl.program_id(2)

    def attend(n_keys):
        rows = qi * tq + lax.broadcasted_iota(jnp.int32, (tq, n_keys), 0)
        causal = lax.broadcasted_iota(jnp.int32, (tq, n_keys), 1) <= rows
        probs = []
        for r in range(2):
            kr = k_ref[0, 0:n_keys, r * dh:(r + 1) * dh].astype(BF16)
            s = lax.dot_general(q[:, r * dh:(r + 1) * dh].astype(BF16), kr, NT,
                                preferred_element_type=F32) * dh ** -0.5
            s = jnp.where(causal, s, -jnp.inf)
            e = jnp.exp(s - jnp.max(s, axis=1, keepdims=True))
            probs.append(e * (1.0 / jnp.sum(e, axis=1, keepdims=True)))
        a = probs[0] - lam * probs[1]
        o = jnp.dot(a.astype(BF16), v_ref[0, 0:n_keys, :].astype(BF16), preferred_element_type=F32)
        o_ref[...] = (_rms(o, hn_ref[...]) * (1.0 - lam_init)).astype(o_ref.dtype)

    for i in range(n_q):
        n_keys = min(lp, -(-(i + 1) * tq // LANES) * LANES)
        pl.when(qi == i)(functools.partial(attend, n_keys))


def _attn_prompt(q, kpad, vpad, lam_vec, head_norm, *, n_batch, seq, n_heads, dh, lam_init):
    lp = kpad.shape[1]
    tq = _pick(seq, (344, 256, 128, 64, 16))
    nq = seq // tq
    est = 4 * lp * 2 * dh * 4 + 4 * tq * 2 * dh * 4 + 8 * tq * lp * 4
    return pl.pallas_call(
        functools.partial(_attn_prompt_kernel, lam_init=lam_init, n_q=nq),
        grid=(n_batch, n_heads, nq),
        in_specs=[pl.BlockSpec((tq, 2 * dh), lambda b, h, i: (b * nq + i, h)),
                  pl.BlockSpec((1, lp, 2 * dh), lambda b, h, i: (b, 0, h)),
                  pl.BlockSpec((1, lp, 2 * dh), lambda b, h, i: (b, 0, h)),
                  pl.BlockSpec((4, dh), lambda b, h, i: (0, 0)),
                  pl.BlockSpec((1, 2 * dh), lambda b, h, i: (0, 0))],
        out_specs=pl.BlockSpec((tq, 2 * dh), lambda b, h, i: (b * nq + i, h)),
        out_shape=jax.ShapeDtypeStruct((n_batch * seq, n_heads * 2 * dh), BF16),
        compiler_params=_cparams(("arbitrary",) * 3, est), name="attn_prompt")(
            q, kpad, vpad, lam_vec, head_norm.reshape(1, 2 * dh).astype(F32))


def _attn_paged_kernel(pt_ref, q_ref, *refs, n_heads, dh, dec_seq, lam_init, pps):
    n_kg, n_vg = 2 * n_heads // HEAD_GROUP, n_heads // HEAD_GROUP
    n_k, n_v = pps * n_kg, pps * 2 * n_vg
    k_refs, v_refs = refs[:n_k], refs[n_k:n_k + n_v]
    kn_ref, vn_ref, lam_ref, hn_ref, o_ref, qbd, m_s, d_s, acc, kflat, vflat = refs[n_k + n_v:]
    p = pl.program_id(1)
    n_rows = qbd.shape[0]
    tq = n_rows // (2 * n_heads)
    grp = 2 * tq

    @pl.when(p == 0)
    def _():
        q8 = q_ref[...] * dh ** -0.5
        if tq < SAMPLE_ROWS:
            first = lax.broadcasted_iota(jnp.int32, q8.shape, 0) < tq
            q8 = jnp.where(first, q8, pltpu.roll(q8, tq, 0))
        qt = jnp.concatenate([q8] * (n_rows // SAMPLE_ROWS), axis=0)
        rg = lax.broadcasted_iota(jnp.int32, qt.shape, 0) >> (tq.bit_length() - 1)
        cg = lax.broadcasted_iota(jnp.int32, qt.shape, 1) >> (dh.bit_length() - 1)
        qbd[...] = jnp.where(rg == cg, qt, 0.0).astype(BF16)
        m_s[...] = jnp.full_like(m_s, -jnp.inf)
        d_s[...] = jnp.zeros_like(d_s)
        acc[...] = jnp.zeros_like(acc)

    def merge(kt, vt, valid):
        s = lax.dot_general(qbd[...], kt.astype(BF16), NT, preferred_element_type=F32)
        if valid is not None:
            s = jnp.where(valid, s, -jnp.inf)
        m_old = m_s[...]
        m_new = jnp.maximum(m_old, jnp.max(s, axis=1, keepdims=True))
        corr = jnp.exp(m_old - m_new)
        pe = jnp.exp(s - m_new)
        d_s[...] = d_s[...] * corr + jnp.sum(pe, axis=1, keepdims=True)
        m_s[...] = m_new
        res = jnp.dot(pe.astype(BF16), vt.astype(BF16), preferred_element_type=F32)
        for h in range(n_heads):
            r = slice(h * grp, (h + 1) * grp)
            acc[r, :] = acc[r, :] * corr[r] + res[r, h * 2 * dh:(h + 1) * 2 * dh]

    page = kflat.shape[0] // pps
    for pg in range(pps):
        keys = slice(pg * page, (pg + 1) * page)
        for c in range(n_kg):
            rows = k_refs[pg * n_kg + c].reshape(page * HEAD_GROUP, dh)
            for i in range(HEAD_GROUP):
                g = c * HEAD_GROUP + i
                kflat[keys, g * dh:(g + 1) * dh] = rows[pl.ds(i, page, stride=HEAD_GROUP), :].astype(BF16)
        for c in range(2 * n_vg):
            rows = v_refs[pg * 2 * n_vg + c].reshape(page * HEAD_GROUP, dh)
            for i in range(HEAD_GROUP):
                col = 2 * ((c // 2) * HEAD_GROUP + i) + c % 2
                vflat[keys, col * dh:(col + 1) * dh] = rows[pl.ds(i, page, stride=HEAD_GROUP), :].astype(BF16)
    merge(kflat[...], vflat[...], None)

    @pl.when(p == pl.num_programs(1) - 1)
    def _():
        zeros = jnp.zeros_like(kn_ref[...])
        kn = jnp.concatenate([kn_ref[...], zeros], axis=0)
        vn = jnp.concatenate([vn_ref[...], zeros], axis=0)
        tok = lax.broadcasted_iota(jnp.int32, (n_rows, 2 * SAMPLE_ROWS), 0) & (tq - 1)
        key = lax.broadcasted_iota(jnp.int32, (n_rows, 2 * SAMPLE_ROWS), 1)
        merge(kn, vn, jnp.logical_and(key <= tok, key < dec_seq))
        lam = _lambda(lam_ref, lam_init)
        o = acc[...] / d_s[...]
        if tq < SAMPLE_ROWS:
            o_ref[...] = jnp.zeros_like(o_ref)
        for h in range(n_heads):
            oh = o[h * grp:h * grp + tq] - lam * o[h * grp + tq:(h + 1) * grp]
            o_ref[0:tq, h * 2 * dh:(h + 1) * 2 * dh] = _rms(oh, hn_ref[...]) * (1.0 - lam_init)


def _attn_paged(q, k, v, cache_k, cache_v, page_table, layer, lam_vec, head_norm, *, row0, n_heads, dh,
                dec_seq, lam_init):
    n_dec, n_pages = page_table.shape
    n_pool, page = cache_k.shape[1], cache_k.shape[2]
    dm = n_heads * 2 * dh
    assert n_heads % HEAD_GROUP == 0
    ck = cache_k.reshape(-1, page, 2 * n_heads, dh)
    cv = cache_v.reshape(-1, page, n_heads, 2 * dh)
    rb0 = row0 // SAMPLE_ROWS
    tq = SAMPLE_ROWS // 2 if dec_seq <= SAMPLE_ROWS // 2 else SAMPLE_ROWS
    n_rows = 2 * n_heads * tq
    base = layer * n_pool
    n_kg, n_vg = 2 * n_heads // HEAD_GROUP, n_heads // HEAD_GROUP
    pps = PAGES_PER_STEP if n_pages % PAGES_PER_STEP == 0 else 1

    def page_spec(pg, group, lane_half):
        return pl.BlockSpec((1, page, HEAD_GROUP, dh),
                            lambda b, p, pt: (base + pt[b * n_pages + p * pps + pg], 0, group, lane_half))

    def new_map(b, p, pt):
        return (rb0 + b, 0)

    grid_spec = pltpu.PrefetchScalarGridSpec(
        num_scalar_prefetch=1, grid=(n_dec, n_pages // pps),
        in_specs=[pl.BlockSpec((SAMPLE_ROWS, dm), new_map),
                  *[page_spec(pg, c, 0) for pg in range(pps) for c in range(n_kg)],
                  *[page_spec(pg, c // 2, c % 2) for pg in range(pps) for c in range(2 * n_vg)],
                  pl.BlockSpec((SAMPLE_ROWS, dm), new_map), pl.BlockSpec((SAMPLE_ROWS, dm), new_map),
                  pl.BlockSpec((4, dh), lambda b, p, pt: (0, 0)),
                  pl.BlockSpec((1, 2 * dh), lambda b, p, pt: (0, 0))],
        out_specs=pl.BlockSpec((SAMPLE_ROWS, dm), lambda b, p, pt: (b, 0)),
        scratch_shapes=[pltpu.VMEM((n_rows, dm), BF16), pltpu.VMEM((n_rows, 1), F32),
                        pltpu.VMEM((n_rows, 1), F32), pltpu.VMEM((n_rows, 2 * dh), F32),
                        pltpu.VMEM((pps * page, dm), BF16), pltpu.VMEM((pps * page, dm), BF16)])
    est = pps * (4 * page * dm * 4 + 3 * page * dm * 2) + 3 * n_rows * dm * 4
    return pl.pallas_call(
        functools.partial(_attn_paged_kernel, n_heads=n_heads, dh=dh, dec_seq=dec_seq, lam_init=lam_init,
                          pps=pps),
        grid_spec=grid_spec, out_shape=jax.ShapeDtypeStruct((n_dec * SAMPLE_ROWS, dm), F32),
        compiler_params=_cparams(("arbitrary", "arbitrary"), est), name="attn_paged")(
            page_table.reshape(-1).astype(jnp.int32), q, *[ck] * (pps * n_kg), *[cv] * (pps * 2 * n_vg), k, v,
            lam_vec,
            head_norm.reshape(1, 2 * dh).astype(F32))


def _moe_plan(top_idx, top_prob, n_exp, tm):
    n, k = top_idx.shape
    e_flat = top_idx.reshape(-1)
    order = jnp.argsort(e_flat, stable=True)
    sorted_e = e_flat[order]
    counts = jnp.sum((e_flat[:, None] == jnp.arange(n_exp, dtype=jnp.int32)[None, :]).astype(jnp.int32), axis=0)
    padded = (counts + tm - 1) // tm * tm
    p_end = jnp.cumsum(padded)
    p_start = p_end - padded
    start = jnp.cumsum(counts) - counts
    dest = (p_start[sorted_e] + jnp.arange(n * k, dtype=jnp.int32) - start[sorted_e]).astype(jnp.int32)
    n_tiles_max = (n * k + n_exp * (tm - 1)) // tm
    rows = n_tiles_max * tm
    src_tok = jnp.zeros((rows,), jnp.int32).at[dest].set((order // k).astype(jnp.int32))
    scale = jnp.zeros((rows,), F32).at[dest].set(top_prob.reshape(-1)[order])
    pos = jnp.zeros((n * k,), jnp.int32).at[order].set(dest).reshape(n, k)
    n_tiles = (p_end[-1] // tm).astype(jnp.int32)
    tile_e = jnp.searchsorted(p_end, jnp.arange(n_tiles_max, dtype=jnp.int32) * tm, side="right")
    tile_e = jnp.minimum(tile_e, n_exp - 1).astype(jnp.int32)
    last_e = tile_e[jnp.maximum(n_tiles - 1, 0)]
    tile_e = jnp.where(jnp.arange(n_tiles_max) < n_tiles, tile_e, last_e)
    tile_rows = jnp.clip(p_start[tile_e] + counts[tile_e] - jnp.arange(n_tiles_max, dtype=jnp.int32) * tm, 0, tm)
    return src_tok, scale, pos, tile_e, n_tiles.reshape(1), tile_rows.astype(jnp.int32)


def _tiles(R):
    tm = _pick(R, (832, 640, 416, 272, 256, 136, 128, 64, 16))
    return tm


def kernel(x_prompt, x_sample, state_gla, state_hgrn, cache_k, cache_v, page_table, meta_tokens, norm_mix, norm_ffn, final_norm, a_w_in, a_gla_w_gate_up, a_gla_b_gate, a_gla_norm, a_hgrn_lb_logits, a_hgrn_norm, a_w_out, c_w_q, c_w_k, c_w_v, c_lambda_q1, c_lambda_k1, c_lambda_q2, c_lambda_k2, c_head_norm, c_w_o, ffn_w_gate, ffn_w_up, ffn_w_down, moe_w_router, moe_w_gate, moe_w_up, moe_w_down):
    B, seq_in, D = x_prompt.shape
    DB, dec_seq, _ = x_sample.shape
    n_meta = meta_tokens.shape[0]
    L = n_meta + seq_in
    BL = B * L
    R = BL + DB * SAMPLE_ROWS
    depth = norm_mix.shape[0]
    GH, gdk, gdv = state_gla.shape[2:]
    HH, hdk, hdv = state_hgrn.shape[2:]
    rank = a_gla_w_gate_up.shape[1]
    n_heads = cache_v.shape[3]
    dh = cache_k.shape[4]
    n_exp = moe_w_router.shape[2]
    past_len = page_table.shape[1] * cache_k.shape[2]
    assert dec_seq <= SAMPLE_ROWS and dh == LANES and BL % SAMPLE_ROWS == 0
    tm = _tiles(R)

    meta = jnp.broadcast_to(meta_tokens[None].astype(F32), (B, n_meta, D))
    xp = jnp.concatenate([meta, x_prompt], axis=1).reshape(BL, D)
    xs = jnp.pad(x_sample, ((0, 0), (0, SAMPLE_ROWS - dec_seq), (0, 0))).reshape(DB * SAMPLE_ROWS, D)
    x = jnp.concatenate([xp, xs], axis=0)

    pos_s = jnp.where(jnp.arange(SAMPLE_ROWS) < dec_seq, past_len + jnp.arange(SAMPLE_ROWS), 0)
    pos = jnp.concatenate([jnp.tile(jnp.arange(L, dtype=jnp.int32), B),
                           jnp.tile(pos_s.astype(jnp.int32), DB)])
    inv = ROPE_THETA ** (-jnp.arange(dh // 2, dtype=F32) * 2.0 / dh)
    ang = pos.astype(F32)[:, None] * inv[None, :]
    rope = (jnp.concatenate([jnp.cos(ang), jnp.cos(ang)], axis=1),
            jnp.concatenate([-jnp.sin(ang), jnp.sin(ang)], axis=1))

    c_low = 2 * GH * gdk + GH * gdv
    deltas = []
    gla_p, gla_s, hgrn_p, hgrn_s, k_p, k_s, v_p, v_s = ([] for _ in range(8))
    for i in range(depth):
        j = i // 2
        if deltas:
            x, xn = _norm(x, deltas, norm_mix[i], want_x=True, out_dtype=BF16)
        else:
            (xn,) = _norm(x, [], norm_mix[i], want_x=False, out_dtype=BF16)
        if i % 2 == 0:
            w_in = a_w_in[j]
            pa = _gmm(xn, [w_in], tm=tm, tn=512 if c_low % 512 == 0 else LANES, n_out=c_low, name="in_a")
            glow = _gmm(xn, [w_in[:, c_low:c_low + rank]], tm=tm, tn=rank, name="in_low")
            w_tail = w_in[:, c_low + rank:]
            pb = _gmm(xn, [w_tail], tm=tm, tn=512 if w_tail.shape[1] % 512 == 0 else LANES, name="in_b")
            wgu = a_gla_w_gate_up[j].astype(F32)
            bg = a_gla_b_gate[j].reshape(1, -1).astype(F32)
            c_hq = GH * gdv
            c_hf, c_hi, c_ho = c_hq + HH * hdk, c_hq + 2 * HH * hdk, c_hq + 2 * HH * hdk + HH * hdv
            hp_h = 4 if HH % 4 == 0 else 1
            outs = []
            for (n_b, n_rows, row0, n_valid, s_gla, s_hgrn, odt) in (
                    (B, L, 0, L, None, None, BF16),
                    (DB, SAMPLE_ROWS, BL, dec_seq, jnp.swapaxes(state_gla[j], -1, -2),
                     jnp.swapaxes(state_hgrn[j], -1, -2), F32)):
                gsrc = [("head", pa, gdk, 0), ("head", pa, gdk, GH), ("head", pa, gdv, 2 * GH * gdk // gdv),
                        ("rows", glow), ("cols", wgu, gdk), ("cols", bg, gdk), ("head", pb, gdv, 0)]
                oa, sa = _recurrence("gla", gsrc, a_gla_norm[j], s_gla, n_batch=n_b, n_heads=GH, hp=1,
                                     n_rows=n_rows, row0=row0, n_valid=n_valid, dk=gdk, dv=gdv, out_dtype=odt)
                hsrc = [("head", pb, hdk, c_hq // hdk), ("head", pb, hdk, c_hf // hdk),
                        ("head", pb, hdv, c_hi // hdv), ("cols", a_hgrn_lb_logits.astype(F32), hdk),
                        ("head", pb, hdv, c_ho // hdv)]
                ob, sb = _recurrence("hgrn", hsrc, a_hgrn_norm[j], s_hgrn, n_batch=n_b, n_heads=HH, hp=hp_h,
                                     n_rows=n_rows, row0=row0, n_valid=n_valid, dk=hdk, dv=hdv, out_dtype=odt,
                                     slot=j)
                outs.append((oa, ob, jnp.swapaxes(sa, -1, -2), jnp.swapaxes(sb, -1, -2)))
            gla_p.append(outs[0][2]); hgrn_p.append(outs[0][3])
            gla_s.append(outs[1][2]); hgrn_s.append(outs[1][3])
            o = jnp.concatenate([jnp.concatenate([outs[0][0], outs[0][1]], axis=1),
                                 jnp.concatenate([outs[1][0], outs[1][1]], axis=1).astype(BF16)], axis=0)
            mix = _gmm(o, [a_w_out[j]], tm=tm, tn=512 if D % 512 == 0 else LANES, name="w_out")
            x, xn = _norm(x, [mix], norm_ffn[i], want_x=True, out_dtype=BF16)
            dff = ffn_w_gate.shape[2]
            hmid = _gmm(xn, [ffn_w_gate[j], ffn_w_up[j]], tm=tm, tn=256 if dff % 256 == 0 else LANES,
                        epilogue="swiglu", out_dtype=BF16, name="ffn_up")
            down = dict(tm=tm // 2 if tm % 32 == 0 else tm, tn=512 if D % 512 == 0 else LANES, k_blk=dff // 2)
            part = _gmm(hmid, [ffn_w_down[j]], kb=0, name="ffn_down0", **down)
            ffn = _gmm(hmid, [ffn_w_down[j]], kb=1, partial=part, name="ffn_down1", **down)
            deltas = [ffn]
        else:
            lam_init = 0.8 - 0.6 * math.exp(-0.3 * i)
            tn = 512 if D % 512 == 0 else LANES
            q = _gmm(xn, [c_w_q[j]], tm=tm, tn=tn, epilogue="rope", rope=rope, name="w_q")
            k = _gmm(xn, [c_w_k[j]], tm=tm, tn=tn, epilogue="rope", rope=rope, name="w_k")
            v = _gmm(xn, [c_w_v[j]], tm=tm, tn=tn, name="w_v")
            k_p.append(k[:BL].reshape(B, L, 2 * n_heads, dh))
            v_p.append(v[:BL].reshape(B, L, n_heads, 2 * dh))
            k_s.append(k[BL:].reshape(DB, SAMPLE_ROWS, 2 * n_heads, dh)[:, :dec_seq])
            v_s.append(v[BL:].reshape(DB, SAMPLE_ROWS, n_heads, 2 * dh)[:, :dec_seq])
            lam_vec = jnp.stack([c_lambda_q1[j], c_lambda_k1[j], c_lambda_q2[j], c_lambda_k2[j]]).astype(F32)
            lp = -(-L // LANES) * LANES
            kpad = jnp.pad(k[:BL].reshape(B, L, D), ((0, 0), (0, lp - L), (0, 0)))
            vpad = jnp.pad(v[:BL].reshape(B, L, D), ((0, 0), (0, lp - L), (0, 0)))
            o_p = _attn_prompt(q, kpad, vpad, lam_vec, c_head_norm[j], n_batch=B, seq=L, n_heads=n_heads,
                               dh=dh, lam_init=lam_init)
            o_s = _attn_paged(q, k, v, cache_k, cache_v, page_table, j, lam_vec, c_head_norm[j], row0=BL,
                              n_heads=n_heads, dh=dh, dec_seq=dec_seq, lam_init=lam_init)
            o = jnp.concatenate([o_p, o_s.astype(BF16)], axis=0)
            mix = _gmm(o, [c_w_o[j]], tm=tm, tn=tn, name="w_o")
            x, xn, ti, tp = _norm(x, [mix], norm_ffn[i], want_x=True, out_dtype=BF16, w_router=moe_w_router[j])
            tm_e = MOE_ROW_TILE
            src_tok, scale, pos_e, tile_e, n_tiles, tile_rows = _moe_plan(
                ti[:, :2].astype(jnp.int32), tp[:, :2], n_exp, tm_e)
            xs_e = jnp.take(xn, src_tok, axis=0)
            eff = moe_w_gate.shape[3]
            routed = dict(tm=tm_e, te=tile_e, nv=n_tiles, tv=tile_rows)
            h_e = _gmm(xs_e, [moe_w_gate[j], moe_w_up[j]], tn=512 if eff % 512 == 0 else LANES,
                       epilogue="swiglu", out_dtype=BF16, name="moe_up", **routed)
            kh = eff // 2
            tn_d = 512 if D % 512 == 0 else LANES
            part = _gmm(h_e, [moe_w_down[j]], tn=tn_d, kb=0, k_blk=kh, name="moe_down0", **routed)
            y_e = _gmm(h_e, [moe_w_down[j]], tn=tn_d, kb=1, k_blk=kh, partial=part, scale=scale,
                       name="moe_down1", **routed)
            deltas = [jnp.take(y_e, pos_e[:, 0], axis=0), jnp.take(y_e, pos_e[:, 1], axis=0)]
    (y,) = _norm(x, deltas, final_norm, want_x=False, out_dtype=F32)
    y_prompt = y[:BL].reshape(B, L, D)[:, n_meta:]
    y_sample = y[BL:].reshape(DB, SAMPLE_ROWS, D)[:, :dec_seq]
    return (y_prompt, y_sample, jnp.stack(gla_p), jnp.stack(hgrn_p), jnp.stack(k_p), jnp.stack(v_p),
            jnp.stack(gla_s), jnp.stack(hgrn_s), jnp.stack(k_s), jnp.stack(v_s))
```
